```python
import math
import jax, jax.numpy as jnp
from jax import lax
import numpy as np

D_MODEL = 1024
BATCH = 16
SEQ = 2048
DEPTH = 2

MEM_LEN = 256
D_MIX = D_MODEL
GM_WIDTH = D_MIX // 2
CHUNK = 128
GM_HEAD_DIM = 128
GM_HEADS = GM_WIDTH // GM_HEAD_DIM
SSM_WIDTH = D_MIX // 4
SSM_GROUP = 16
SSM_GROUPS = SSM_WIDTH // SSM_GROUP
SSM_STATE = 64
XA_WIDTH = D_MIX - GM_WIDTH - SSM_WIDTH
XA_HEADS = 4
XA_HEAD_DIM = XA_WIDTH // XA_HEADS
IN_WIDTH = 3 * GM_WIDTH + 2 * SSM_WIDTH + 2 * XA_WIDTH
SPLITS = (GM_WIDTH, 2 * GM_WIDTH, 3 * GM_WIDTH,
          3 * GM_WIDTH + SSM_WIDTH, 3 * GM_WIDTH + 2 * SSM_WIDTH,
          3 * GM_WIDTH + 2 * SSM_WIDTH + XA_WIDTH)
DN_ALPHA = (2 * DEPTH) ** 0.25
DN_BETA = (8 * DEPTH) ** -0.25
LN_EPS = 1e-5
DT_MIN = 0.001
DT_MAX = 0.1

kernel_name = "hybrid_gmlp_s5_memxattn_deepnorm"


def layer_norm(x, g, b):
    xf = x.astype(jnp.float32)
    mu = jnp.mean(xf, axis=-1, keepdims=True)
    xc = xf - mu
    var = jnp.mean(xc * xc, axis=-1, keepdims=True)
    return (xc * lax.rsqrt(var + LN_EPS) * g.astype(jnp.float32) + b.astype(jnp.float32)).astype(x.dtype)


def spatial_gating(u, v, w_s, b_s, ln_g, ln_b, causal):
    Bsz, L, _ = v.shape
    nc = L // CHUNK
    vh = layer_norm(v.reshape(Bsz, L, GM_HEADS, GM_HEAD_DIM), ln_g, ln_b)
    w = jnp.where(causal[None], w_s, 0)
    vc = vh.reshape(Bsz, nc, CHUNK, GM_HEADS, GM_HEAD_DIM)
    mixed = jnp.einsum('hts,bcshd->bcthd', w, vc) + b_s.T[None, None, :, :, None]
    return u * mixed.reshape(Bsz, L, GM_WIDTH)


def _diag_combine(c1, c2):
    a1, b1 = c1
    a2, b2 = c2
    return a1 * a2, a2 * b1 + b2


def s5_branch(xs, lam_re, lam_im, log_step, b_re, b_im, c_re, c_im, d_skip, glu_w, glu_b):
    f32 = jnp.float32
    Bsz, L, W = xs.shape
    xg = xs.reshape(Bsz, L, SSM_GROUPS, SSM_GROUP).astype(f32)
    lam = lax.complex(lam_re.astype(f32), lam_im.astype(f32))
    step = jnp.exp(log_step.astype(f32))[:, None]
    lam_bar = jnp.exp(lam * step)
    b_mat = lax.complex(b_re.astype(f32), b_im.astype(f32))
    b_bar = ((lam_bar - 1.0) / lam)[:, :, None] * b_mat
    c_mat = lax.complex(c_re.astype(f32), c_im.astype(f32))
    bu = jnp.einsum('gpc,blgc->blgp', b_bar, xg.astype(jnp.complex64))
    decay = jnp.broadcast_to(lam_bar, bu.shape)
    _, h = lax.associative_scan(_diag_combine, (decay, bu), axis=1)
    y = jnp.einsum('gcp,blgp->blgc', c_mat, h).real \
        + d_skip.astype(f32).reshape(SSM_GROUPS, SSM_GROUP) * xg
    y = jax.nn.gelu(y.reshape(Bsz, L, W)).astype(xs.dtype)
    return y * jax.nn.sigmoid(y @ glu_w + glu_b)


def memory_cross_attention(q, mem, w_k, w_v):
    Bsz, L, _ = q.shape
    M = mem.shape[1]
    qh = q.reshape(Bsz, L, XA_HEADS, XA_HEAD_DIM)
    kh = (mem @ w_k).reshape(Bsz, M, XA_HEADS, XA_HEAD_DIM)
    vh = (mem @ w_v).reshape(Bsz, M, XA_HEADS, XA_HEAD_DIM)
    s = jnp.einsum('blhd,bmhd->bhlm', qh, kh, preferred_element_type=jnp.float32)
    p = jax.nn.softmax(s * (XA_HEAD_DIM ** -0.5), axis=-1).astype(vh.dtype)
    o = jnp.einsum('bhlm,bmhd->blhd', p, vh)
    return o.reshape(Bsz, L, XA_WIDTH)


def setup_inputs(seed: int = 0) -> dict:
    key = jax.random.key(seed)
    ks = jax.random.split(key, 24)
    f32 = jnp.float32
    nrm = lambda k, shape, s: jax.random.normal(k, shape, f32) * s
    x = jax.random.normal(ks[0], (BATCH, SEQ, D_MODEL), f32)
    mem = jax.random.normal(ks[1], (BATCH, MEM_LEN, D_MODEL), f32)
    w_in = nrm(ks[2], (DEPTH, D_MODEL, IN_WIDTH), D_MODEL ** -0.5)
    gm_w_s = nrm(ks[3], (DEPTH, GM_HEADS, CHUNK, CHUNK), CHUNK ** -0.5)
    gm_b_s = 1.0 + nrm(ks[4], (DEPTH, GM_HEADS, CHUNK), 0.01)
    gm_ln_g = 1.0 + nrm(ks[5], (DEPTH, GM_HEADS, GM_HEAD_DIM), 0.01)
    gm_ln_b = nrm(ks[6], (DEPTH, GM_HEADS, GM_HEAD_DIM), 0.01)
    n = jnp.arange(SSM_STATE, dtype=f32)
    ssm_lam_re = -0.5 + nrm(ks[7], (DEPTH, SSM_GROUPS, SSM_STATE), 0.01)
    ssm_lam_im = math.pi * n + nrm(ks[8], (DEPTH, SSM_GROUPS, SSM_STATE), 0.01)
    ssm_log_step = jax.random.uniform(ks[9], (DEPTH, SSM_GROUPS), f32,
                                      math.log(DT_MIN), math.log(DT_MAX))
    bs = (2.0 * SSM_GROUP) ** -0.5
    ssm_b_re = nrm(ks[10], (DEPTH, SSM_GROUPS, SSM_STATE, SSM_GROUP), bs)
    ssm_b_im = nrm(ks[11], (DEPTH, SSM_GROUPS, SSM_STATE, SSM_GROUP), bs)
    cs = (2.0 * SSM_STATE) ** -0.5
    ssm_c_re = nrm(ks[12], (DEPTH, SSM_GROUPS, SSM_GROUP, SSM_STATE), cs)
    ssm_c_im = nrm(ks[13], (DEPTH, SSM_GROUPS, SSM_GROUP, SSM_STATE), cs)
    ssm_d = nrm(ks[14], (DEPTH, SSM_WIDTH), 1.0)
    glu_w = nrm(ks[15], (DEPTH, SSM_WIDTH, SSM_WIDTH), SSM_WIDTH ** -0.5)
    glu_b = nrm(ks[16], (DEPTH, SSM_WIDTH), 0.01)
    xa_w_k = nrm(ks[17], (DEPTH, D_MODEL, XA_WIDTH), D_MODEL ** -0.5)
    xa_w_v = nrm(ks[18], (DEPTH, D_MODEL, XA_WIDTH), D_MODEL ** -0.5)
    w_out = nrm(ks[19], (DEPTH, D_MIX, D_MODEL), DN_BETA * D_MIX ** -0.5)
    ln_g = 1.0 + nrm(ks[20], (DEPTH, D_MODEL), 0.01)
    ln_b = nrm(ks[21], (DEPTH, D_MODEL), 0.01)
    return {"x": x, "mem": mem, "w_in": w_in, "gm_w_s": gm_w_s, "gm_b_s": gm_b_s,
            "gm_ln_g": gm_ln_g, "gm_ln_b": gm_ln_b, "ssm_lam_re": ssm_lam_re,
            "ssm_lam_im": ssm_lam_im, "ssm_log_step": ssm_log_step,
            "ssm_b_re": ssm_b_re, "ssm_b_im": ssm_b_im, "ssm_c_re": ssm_c_re,
            "ssm_c_im": ssm_c_im, "ssm_d": ssm_d, "glu_w": glu_w, "glu_b": glu_b,
            "xa_w_k": xa_w_k, "xa_w_v": xa_w_v, "w_out": w_out,
            "ln_g": ln_g, "ln_b": ln_b}


def reference(x, mem, w_in, gm_w_s, gm_b_s, gm_ln_g, gm_ln_b, ssm_lam_re, ssm_lam_im,
              ssm_log_step, ssm_b_re, ssm_b_im, ssm_c_re, ssm_c_im, ssm_d, glu_w, glu_b,
              xa_w_k, xa_w_v, w_out, ln_g, ln_b):
    causal = jnp.tril(jnp.ones((CHUNK, CHUNK), dtype=bool))
    for layer in range(DEPTH):
        z = x @ w_in[layer]
        u_a, v_a, g_a, x_b, g_b, q_x, g_x = jnp.split(z, SPLITS, axis=-1)
        y_a = spatial_gating(jax.nn.gelu(u_a), jax.nn.gelu(v_a), gm_w_s[layer], gm_b_s[layer],
                             gm_ln_g[layer], gm_ln_b[layer], causal)
        y_b = s5_branch(x_b, ssm_lam_re[layer], ssm_lam_im[layer], ssm_log_step[layer],
                        ssm_b_re[layer], ssm_b_im[layer], ssm_c_re[layer], ssm_c_im[layer],
                        ssm_d[layer], glu_w[layer], glu_b[layer])
        y_x = memory_cross_attention(q_x, mem, xa_w_k[layer], xa_w_v[layer])
        y = jnp.concatenate([y_a * jax.nn.silu(g_a),
                             y_b * jax.nn.silu(g_b),
                             y_x * jax.nn.silu(g_x)], axis=-1)
        x = layer_norm(DN_ALPHA * x + y @ w_out[layer], ln_g[layer], ln_b[layer])
    return x
```

```python
import functools
import math

import jax
import jax.numpy as jnp
from jax import lax
from jax.experimental import pallas as pl
from jax.experimental.pallas import tpu as pltpu

F32 = jnp.float32
BF16 = jnp.bfloat16

D_MODEL = 1024
GM_WIDTH = 512
GM_HEADS = 4
GM_HEAD_DIM = 128
CHUNK = 128
SSM_WIDTH = 256
SSM_GROUP = 16
SSM_GROUPS = 16
SSM_STATE = 64
N_STATE = SSM_GROUPS * SSM_STATE
XA_WIDTH = 256
XA_HEADS = 4
XA_HEAD_DIM = 64
IN_WIDTH = 3 * GM_WIDTH + 2 * SSM_WIDTH + 2 * XA_WIDTH
LN_EPS = 1e-5

C_U, C_V, C_GA = 0, GM_WIDTH, 2 * GM_WIDTH
C_XB = 3 * GM_WIDTH
C_GB = C_XB + SSM_WIDTH
C_Q = C_GB + SSM_WIDTH
C_GX = C_Q + XA_WIDTH

BLOCK_B = 8
BLOCK_T = CHUNK
ROWS = BLOCK_B * BLOCK_T
LANES = 128
PERM_PITCH = BLOCK_T + 8
VMEM_LIMIT_BYTES = 60 * 1024 * 1024

_GELU_C = math.sqrt(2.0 / math.pi)


def _gelu(x):
    return 0.5 * x * (1.0 + jnp.tanh(_GELU_C * (x + 0.044715 * (x * x * x))))


def _sigmoid(x):
    return 1.0 / (1.0 + jnp.exp(-x))


def _silu(x):
    return x * _sigmoid(x)


def _dot(a, b):
    return jnp.dot(a, b, preferred_element_type=F32)


def _kv_kernel(mem_ref, wkt_ref, wv_ref, kt_ref, v_ref):
    m16 = mem_ref[0].astype(BF16)
    kt = lax.dot_general(wkt_ref[...], m16, (((1,), (1,)), ((), ())),
                         preferred_element_type=F32)
    kt_ref[0] = kt.astype(BF16)
    v_ref[0] = _dot(m16, wv_ref[...]).astype(BF16)


def _layer_kernel(alpha,
                  x_ref, kt_ref, v_ref, w_in_ref, w_out_ref, wb_ref, wc_ref, gluw_ref,
                  gmw_ref, gmbias_ref, gmg_ref, gmb_ref, lamr_ref, lami_ref, d_ref,
                  glub_ref, lng_ref, lnb_ref,
                  o_ref,
                  y_ref, perm_ref, xbp_ref, bu_ref, gateb_ref, h_ref):
    j = pl.program_id(1)

    @pl.when(j == 0)
    def _():
        h_ref[...] = jnp.zeros_like(h_ref)

    tri = (lax.broadcasted_iota(jnp.int32, (CHUNK, CHUNK), 0)
           >= lax.broadcasted_iota(jnp.int32, (CHUNK, CHUNK), 1))
    row_head = lax.broadcasted_iota(jnp.int32, (XA_WIDTH, XA_WIDTH), 0) // XA_HEAD_DIM
    col_head = lax.broadcasted_iota(jnp.int32, (XA_WIDTH, XA_WIDTH), 1) // XA_HEAD_DIM

    def phase1(b, carry):
        r0 = pl.multiple_of(b * BLOCK_T, BLOCK_T)
        x16 = x_ref[b].astype(BF16)
        z = _dot(x16, w_in_ref[...])

        xs = z[:, C_XB:C_XB + SSM_WIDTH]
        p0 = pl.multiple_of(b * PERM_PITCH, 8)
        for s in range(SSM_WIDTH // LANES):
            perm_ref[s, pl.ds(p0, BLOCK_T), :] = xs[:, s * LANES:(s + 1) * LANES]
        gateb_ref[pl.ds(r0, BLOCK_T), :] = _silu(z[:, C_GB:C_GB + SSM_WIDTH])

        u = _gelu(z[:, C_U:C_U + GM_WIDTH])
        v = _gelu(z[:, C_V:C_V + GM_WIDTH])
        sga = _silu(z[:, C_GA:C_GA + GM_WIDTH])
        ya = []
        for h in range(GM_HEADS):
            sl = slice(h * GM_HEAD_DIM, (h + 1) * GM_HEAD_DIM)
            vh = v[:, sl]
            mu = jnp.mean(vh, axis=-1, keepdims=True)
            xc = vh - mu
            var = jnp.mean(xc * xc, axis=-1, keepdims=True)
            vn = xc * lax.rsqrt(var + LN_EPS) * gmg_ref[:, sl] + gmb_ref[:, sl]
            w = jnp.where(tri, gmw_ref[h], 0.0).astype(BF16)
            mixed = _dot(w, vn.astype(BF16)) + gmbias_ref[h]
            ya.append(u[:, sl] * mixed * sga[:, sl])
        y_ref[pl.ds(r0, BLOCK_T), 0:GM_WIDTH] = jnp.concatenate(ya, axis=1).astype(BF16)

        q16 = (z[:, C_Q:C_Q + XA_WIDTH] * (XA_HEAD_DIM ** -0.5)).astype(BF16)
        kt = kt_ref[b]
        vv = v_ref[b]
        acc = jnp.zeros((BLOCK_T, 2 * XA_WIDTH), F32)
        for h in range(XA_HEADS):
            kth = jnp.where(row_head == h, kt, jnp.zeros_like(kt))
            s = _dot(q16, kth)
            m = jnp.max(s, axis=-1, keepdims=True)
            p = jnp.exp(s - m).astype(BF16)
            vh = jnp.where(col_head == h, vv, jnp.zeros_like(vv))
            ones_h = jnp.where(col_head == h, 1.0, 0.0).astype(BF16)
            acc = acc + _dot(p, jnp.concatenate([vh, ones_h], axis=1))
        yx = acc[:, :XA_WIDTH] / acc[:, XA_WIDTH:]
        yx = yx * _silu(z[:, C_GX:C_GX + XA_WIDTH])
        y_ref[pl.ds(r0, BLOCK_T), GM_WIDTH + SSM_WIDTH:] = yx.astype(BF16)
        return carry

    lax.fori_loop(0, BLOCK_B, phase1, 0)

    n_slab = SSM_WIDTH // LANES

    def to_time_major(t, carry):
        r0 = pl.multiple_of(t * BLOCK_B, BLOCK_B)
        for s in range(n_slab):
            xbp_ref[pl.ds(r0, BLOCK_B), s * LANES:(s + 1) * LANES] = (
                perm_ref[s, pl.ds(t, BLOCK_B, stride=PERM_PITCH), :])
        return carry

    lax.fori_loop(0, BLOCK_T, to_time_major, 0, unroll=8)

    xbp = xbp_ref[...]
    bu_ref[...] = _dot(xbp.astype(BF16), wb_ref[...])

    lam_r = jnp.broadcast_to(lamr_ref[...], (BLOCK_B, N_STATE))
    lam_i = jnp.broadcast_to(lami_ref[...], (BLOCK_B, N_STATE))

    def scan_step(t, carry):
        hr, hi = carry
        r0 = pl.multiple_of(t * BLOCK_B, BLOCK_B)
        br = bu_ref[pl.ds(r0, BLOCK_B), 0:N_STATE]
        bi = bu_ref[pl.ds(r0, BLOCK_B), N_STATE:]
        nr = lam_r * hr - lam_i * hi + br
        ni = lam_r * hi + lam_i * hr + bi
        bu_ref[pl.ds(r0, BLOCK_B), 0:N_STATE] = nr
        bu_ref[pl.ds(r0, BLOCK_B), N_STATE:] = ni
        return nr, ni

    hr, hi = lax.fori_loop(0, BLOCK_T, scan_step,
                           (h_ref[:, 0:N_STATE], h_ref[:, N_STATE:]), unroll=4)
    h_ref[:, 0:N_STATE] = hr
    h_ref[:, N_STATE:] = hi

    ys = _dot(bu_ref[...].astype(BF16), wc_ref[...]) + d_ref[...] * xbp
    yg = _gelu(ys)
    yb = yg * _sigmoid(_dot(yg.astype(BF16), gluw_ref[...]) + glub_ref[...])
    xbp_ref[...] = yb

    def to_batch_major(t, carry):
        r0 = pl.multiple_of(t * BLOCK_B, BLOCK_B)
        for s in range(n_slab):
            perm_ref[s, pl.ds(t, BLOCK_B, stride=PERM_PITCH), :] = (
                xbp_ref[pl.ds(r0, BLOCK_B), s * LANES:(s + 1) * LANES])
        return carry

    lax.fori_loop(0, BLOCK_T, to_batch_major, 0, unroll=8)

    for b in range(BLOCK_B):
        ybb = jnp.concatenate(
            [perm_ref[s, b * PERM_PITCH:b * PERM_PITCH + BLOCK_T, :] for s in range(n_slab)],
            axis=1)
        rows = slice(b * BLOCK_T, (b + 1) * BLOCK_T)
        y_ref[rows, GM_WIDTH:GM_WIDTH + SSM_WIDTH] = (ybb * gateb_ref[rows, :]).astype(BF16)

    def phase3(b, carry):
        r0 = pl.multiple_of(b * BLOCK_T, BLOCK_T)
        out = _dot(y_ref[pl.ds(r0, BLOCK_T), :], w_out_ref[...])
        r = alpha * x_ref[b] + out
        mu = jnp.mean(r, axis=-1, keepdims=True)
        rc = r - mu
        var = jnp.mean(rc * rc, axis=-1, keepdims=True)
        o_ref[b] = rc * lax.rsqrt(var + LN_EPS) * lng_ref[...] + lnb_ref[...]
        return carry

    lax.fori_loop(0, BLOCK_B, phase3, 0)


def _full(shape):
    nd = len(shape)
    return pl.BlockSpec(shape, lambda i, j, _nd=nd: (0,) * _nd)


def _kv_project(mem, wkt16, wv16):
    bsz, m_len, d = mem.shape
    out_sds = jax.ShapeDtypeStruct((bsz, XA_WIDTH, m_len), BF16)
    v_sds = jax.ShapeDtypeStruct((bsz, m_len, XA_WIDTH), BF16)
    return pl.pallas_call(
        _kv_kernel,
        grid=(bsz,),
        in_specs=[pl.BlockSpec((1, m_len, d), lambda i: (i, 0, 0)),
                  pl.BlockSpec((XA_WIDTH, d), lambda i: (0, 0)),
                  pl.BlockSpec((d, XA_WIDTH), lambda i: (0, 0))],
        out_specs=[pl.BlockSpec((1, XA_WIDTH, m_len), lambda i: (i, 0, 0)),
                   pl.BlockSpec((1, m_len, XA_WIDTH), lambda i: (i, 0, 0))],
        out_shape=[out_sds, v_sds],
        name="kv_project",
    )(mem, wkt16, wv16)


def _layer(x, kt, vv, p, alpha):
    bsz, seq, d = x.shape
    m_len = kt.shape[2]
    grid = (bsz // BLOCK_B, seq // BLOCK_T)
    in_specs = [
        pl.BlockSpec((BLOCK_B, BLOCK_T, d), lambda i, j: (i, j, 0)),
        pl.BlockSpec((BLOCK_B, XA_WIDTH, m_len), lambda i, j: (i, 0, 0)),
        pl.BlockSpec((BLOCK_B, m_len, XA_WIDTH), lambda i, j: (i, 0, 0)),
        _full((d, IN_WIDTH)), _full((d, d)),
        _full((SSM_WIDTH, 2 * N_STATE)), _full((2 * N_STATE, SSM_WIDTH)),
        _full((SSM_WIDTH, SSM_WIDTH)),
        _full((GM_HEADS, CHUNK, CHUNK)), _full((GM_HEADS, CHUNK, GM_HEAD_DIM)),
        _full((1, GM_WIDTH)), _full((1, GM_WIDTH)),
        _full((1, N_STATE)), _full((1, N_STATE)), _full((1, SSM_WIDTH)),
        _full((1, SSM_WIDTH)), _full((1, d)), _full((1, d)),
    ]
    scratch = [
        pltpu.VMEM((ROWS, d), BF16),
        pltpu.VMEM((SSM_WIDTH // LANES, BLOCK_B * PERM_PITCH, LANES), F32),
        pltpu.VMEM((ROWS, SSM_WIDTH), F32),
        pltpu.VMEM((ROWS, 2 * N_STATE), F32),
        pltpu.VMEM((ROWS, SSM_WIDTH), F32),
        pltpu.VMEM((BLOCK_B, 2 * N_STATE), F32),
    ]
    return pl.pallas_call(
        functools.partial(_layer_kernel, alpha),
        grid=grid,
        in_specs=in_specs,
        out_specs=pl.BlockSpec((BLOCK_B, BLOCK_T, d), lambda i, j: (i, j, 0)),
        out_shape=jax.ShapeDtypeStruct(x.shape, x.dtype),
        scratch_shapes=scratch,
        compiler_params=pltpu.CompilerParams(
            dimension_semantics=("arbitrary", "arbitrary"),
            vmem_limit_bytes=VMEM_LIMIT_BYTES),
        name="trunk_layer",
    )(x, kt, vv, p["w_in"], p["w_out"], p["wb"], p["wc"], p["glu_w"],
      p["gm_w"], p["gm_bias"], p["gm_g"], p["gm_b"], p["lam_r"], p["lam_i"], p["d"],
      p["glu_b"], p["ln_g"], p["ln_b"])


def _ssm_discretize(lam_re, lam_im, log_step, b_re, b_im, c_re, c_im):
    step = jnp.exp(log_step)[:, None]
    mag = jnp.exp(lam_re * step)
    lbr = mag * jnp.cos(lam_im * step)
    lbi = mag * jnp.sin(lam_im * step)
    den = lam_re * lam_re + lam_im * lam_im
    fr = ((lbr - 1.0) * lam_re + lbi * lam_im) / den
    fi = (lbi * lam_re - (lbr - 1.0) * lam_im) / den
    bbr = fr[:, :, None] * b_re - fi[:, :, None] * b_im
    bbi = fr[:, :, None] * b_im + fi[:, :, None] * b_re
    eye = jnp.eye(SSM_GROUPS, dtype=F32)
    wb = jnp.concatenate(
        [jnp.einsum('gpc,gk->gckp', m, eye).reshape(SSM_WIDTH, N_STATE) for m in (bbr, bbi)],
        axis=1)
    wc = jnp.concatenate(
        [jnp.einsum('gcp,gk->gpkc', m, eye).reshape(N_STATE, SSM_WIDTH) for m in (c_re, -c_im)],
        axis=0)
    return lbr.reshape(1, N_STATE), lbi.reshape(1, N_STATE), wb, wc


def kernel(x, mem, w_in, gm_w_s, gm_b_s, gm_ln_g, gm_ln_b, ssm_lam_re, ssm_lam_im,
           ssm_log_step, ssm_b_re, ssm_b_im, ssm_c_re, ssm_c_im, ssm_d, glu_w, glu_b,
           xa_w_k, xa_w_v, w_out, ln_g, ln_b):
    depth = w_in.shape[0]
    alpha = float((2 * depth) ** 0.25)
    for l in range(depth):
        lam_r, lam_i, wb, wc = _ssm_discretize(
            ssm_lam_re[l], ssm_lam_im[l], ssm_log_step[l], ssm_b_re[l], ssm_b_im[l],
            ssm_c_re[l], ssm_c_im[l])
        p = {
            "w_in": w_in[l].astype(BF16), "w_out": w_out[l].astype(BF16),
            "wb": wb.astype(BF16), "wc": wc.astype(BF16), "glu_w": glu_w[l].astype(BF16),
            "gm_w": gm_w_s[l],
            "gm_bias": jnp.broadcast_to(gm_b_s[l][:, :, None], (GM_HEADS, CHUNK, GM_HEAD_DIM)),
            "gm_g": gm_ln_g[l].reshape(1, GM_WIDTH), "gm_b": gm_ln_b[l].reshape(1, GM_WIDTH),
            "lam_r": lam_r, "lam_i": lam_i, "d": ssm_d[l].reshape(1, SSM_WIDTH),
            "glu_b": glu_b[l].reshape(1, SSM_WIDTH),
            "ln_g": ln_g[l].reshape(1, D_MODEL), "ln_b": ln_b[l].reshape(1, D_MODEL),
        }
        kt, vv = _kv_project(mem, xa_w_k[l].T.astype(BF16), xa_w_v[l].astype(BF16))
        x = _layer(x, kt, vv, p, alpha)
    return x
```

```python
import functools
import math

import jax
import jax.numpy as jnp
from jax import lax
from jax.experimental import pallas as pl
from jax.experimental.pallas import tpu as pltpu

F32 = jnp.float32
BF16 = jnp.bfloat16

D_MODEL = 1024
GM_WIDTH = 512
GM_HEADS = 4
GM_HEAD_DIM = 128
CHUNK = 128
SSM_WIDTH = 256
SSM_GROUP = 16
SSM_GROUPS = 16
SSM_STATE = 64
N_STATE = SSM_GROUPS * SSM_STATE
XA_WIDTH = 256
XA_HEADS = 4
XA_HEAD_DIM = 64
IN_WIDTH = 3 * GM_WIDTH + 2 * SSM_WIDTH + 2 * XA_WIDTH
LN_EPS = 1e-5

C_U, C_V, C_GA = 0, GM_WIDTH, 2 * GM_WIDTH
C_XB = 3 * GM_WIDTH
C_GB = C_XB + SSM_WIDTH
C_Q = C_GB + SSM_WIDTH
C_GX = C_Q + XA_WIDTH

BLOCK_B = 8
BLOCK_T = CHUNK
ROWS = BLOCK_B * BLOCK_T
LANES = 128
PERM_PITCH = BLOCK_T + 8
VMEM_LIMIT_BYTES = 60 * 1024 * 1024

_GELU_C = math.sqrt(2.0 / math.pi)


def _gelu(x):
    return 0.5 * x * (1.0 + jnp.tanh(_GELU_C * (x + 0.044715 * (x * x * x))))


def _sigmoid(x):
    return 1.0 / (1.0 + jnp.exp(-x))


def _silu(x):
    return x * _sigmoid(x)


def _dot(a, b):
    return jnp.dot(a, b, preferred_element_type=F32)


def _kv_kernel(mem_ref, wkt_ref, wv_ref, kt_ref, v_ref):
    m16 = mem_ref[0].astype(BF16)
    kt = lax.dot_general(wkt_ref[...], m16, (((1,), (1,)), ((), ())),
                         preferred_element_type=F32)
    kt_ref[0] = kt.astype(BF16)
    v_ref[0] = _dot(m16, wv_ref[...]).astype(BF16)


def _layer_kernel(alpha,
                  x_ref, kt_ref, v_ref, w_in_ref, w_out_ref, wb_ref, wc_ref, gluw_ref,
                  gmw_ref, gmbias_ref, gmg_ref, gmb_ref, lamr_ref, lami_ref, d_ref,
                  glub_ref, lng_ref, lnb_ref,
                  o_ref,
                  y_ref, perm_ref, xbp_ref, bu_ref, gateb_ref, h_ref):
    j = pl.program_id(1)

    @pl.when(j == 0)
    def _():
        h_ref[...] = jnp.zeros_like(h_ref)

    tri = (lax.broadcasted_iota(jnp.int32, (CHUNK, CHUNK), 0)
           >= lax.broadcasted_iota(jnp.int32, (CHUNK, CHUNK), 1))
    row_head = lax.broadcasted_iota(jnp.int32, (XA_WIDTH, XA_WIDTH), 0) // XA_HEAD_DIM
    col_head = lax.broadcasted_iota(jnp.int32, (XA_WIDTH, XA_WIDTH), 1) // XA_HEAD_DIM

    def phase1(b, carry):
        r0 = pl.multiple_of(b * BLOCK_T, BLOCK_T)
        x16 = x_ref[b].astype(BF16)
        z = _dot(x16, w_in_ref[...])

        xs = z[:, C_XB:C_XB + SSM_WIDTH]
        p0 = pl.multiple_of(b * PERM_PITCH, 8)
        for s in range(SSM_WIDTH // LANES):
            perm_ref[s, pl.ds(p0, BLOCK_T), :] = xs[:, s * LANES:(s + 1) * LANES]
        gateb_ref[pl.ds(r0, BLOCK_T), :] = _silu(z[:, C_GB:C_GB + SSM_WIDTH])

        u = _gelu(z[:, C_U:C_U + GM_WIDTH])
        v = _gelu(z[:, C_V:C_V + GM_WIDTH])
        sga = _silu(z[:, C_GA:C_GA + GM_WIDTH])
        ya = []
        for h in range(GM_HEADS):
            sl = slice(h * GM_HEAD_DIM, (h + 1) * GM_HEAD_DIM)
            vh = v[:, sl]
            mu = jnp.mean(vh, axis=-1, keepdims=True)
            xc = vh - mu
            var = jnp.mean(xc * xc, axis=-1, keepdims=True)
            vn = xc * lax.rsqrt(var + LN_EPS) * gmg_ref[:, sl] + gmb_ref[:, sl]
            w = jnp.where(tri, gmw_ref[h], 0.0).astype(BF16)
            mixed = _dot(w, vn.astype(BF16)) + gmbias_ref[h]
            ya.append(u[:, sl] * mixed * sga[:, sl])
        y_ref[pl.ds(r0, BLOCK_T), 0:GM_WIDTH] = jnp.concatenate(ya, axis=1).astype(BF16)

        q16 = (z[:, C_Q:C_Q + XA_WIDTH] * (XA_HEAD_DIM ** -0.5)).astype(BF16)
        kt = kt_ref[b]
        vv = v_ref[b]
        acc = jnp.zeros((BLOCK_T, 2 * XA_WIDTH), F32)
        for h in range(XA_HEADS):
            kth = jnp.where(row_head == h, kt, jnp.zeros_like(kt))
            s = _dot(q16, kth)
            m = jnp.max(s, axis=-1, keepdims=True)
            p = jnp.exp(s - m).astype(BF16)
            vh = jnp.where(col_head == h, vv, jnp.zeros_like(vv))
            ones_h = jnp.where(col_head == h, 1.0, 0.0).astype(BF16)
            acc = acc + _dot(p, jnp.concatenate([vh, ones_h], axis=1))
        yx = acc[:, :XA_WIDTH] / acc[:, XA_WIDTH:]
        yx = yx * _silu(z[:, C_GX:C_GX + XA_WIDTH])
        y_ref[pl.ds(r0, BLOCK_T), GM_WIDTH + SSM_WIDTH:] = yx.astype(BF16)
        return carry

    lax.fori_loop(0, BLOCK_B, phase1, 0, unroll=2)

    n_slab = SSM_WIDTH // LANES

    def to_time_major(t, carry):
        r0 = pl.multiple_of(t * BLOCK_B, BLOCK_B)
        for s in range(n_slab):
            xbp_ref[pl.ds(r0, BLOCK_B), s * LANES:(s + 1) * LANES] = (
                perm_ref[s, pl.ds(t, BLOCK_B, stride=PERM_PITCH), :])
        return carry

    lax.fori_loop(0, BLOCK_T, to_time_major, 0, unroll=8)

    xbp = xbp_ref[...]
    xbp16 = xbp.astype(BF16)
    half = ROWS // 2
    bu_ref[0:half, :] = _dot(xbp16[0:half], wb_ref[...])
    bu_ref[half:, :] = _dot(xbp16[half:], wb_ref[...])

    lam_r = jnp.broadcast_to(lamr_ref[...], (BLOCK_B, N_STATE))
    lam_i = jnp.broadcast_to(lami_ref[...], (BLOCK_B, N_STATE))

    def scan_step(t, carry):
        hr, hi = carry
        r0 = pl.multiple_of(t * BLOCK_B, BLOCK_B)
        br = bu_ref[pl.ds(r0, BLOCK_B), 0:N_STATE]
        bi = bu_ref[pl.ds(r0, BLOCK_B), N_STATE:]
        nr = lam_r * hr - lam_i * hi + br
        ni = lam_r * hi + lam_i * hr + bi
        bu_ref[pl.ds(r0, BLOCK_B), 0:N_STATE] = nr
        bu_ref[pl.ds(r0, BLOCK_B), N_STATE:] = ni
        return nr, ni

    hr, hi = lax.fori_loop(0, BLOCK_T, scan_step,
                           (h_ref[:, 0:N_STATE], h_ref[:, N_STATE:]), unroll=4)
    h_ref[:, 0:N_STATE] = hr
    h_ref[:, N_STATE:] = hi

    ys = _dot(bu_ref[...].astype(BF16), wc_ref[...]) + d_ref[...] * xbp
    yg = _gelu(ys)
    yb = yg * _sigmoid(_dot(yg.astype(BF16), gluw_ref[...]) + glub_ref[...])
    xbp_ref[...] = yb

    def to_batch_major(t, carry):
        r0 = pl.multiple_of(t * BLOCK_B, BLOCK_B)
        for s in range(n_slab):
            perm_ref[s, pl.ds(t, BLOCK_B, stride=PERM_PITCH), :] = (
                xbp_ref[pl.ds(r0, BLOCK_B), s * LANES:(s + 1) * LANES])
        return carry

    lax.fori_loop(0, BLOCK_T, to_batch_major, 0, unroll=8)

    for b in range(BLOCK_B):
        ybb = jnp.concatenate(
            [perm_ref[s, b * PERM_PITCH:b * PERM_PITCH + BLOCK_T, :] for s in range(n_slab)],
            axis=1)
        rows = slice(b * BLOCK_T, (b + 1) * BLOCK_T)
        y_ref[rows, GM_WIDTH:GM_WIDTH + SSM_WIDTH] = (ybb * gateb_ref[rows, :]).astype(BF16)

    def phase3(b, carry):
        r0 = pl.multiple_of(b * BLOCK_T, BLOCK_T)
        out = _dot(y_ref[pl.ds(r0, BLOCK_T), :], w_out_ref[...])
        r = alpha * x_ref[b] + out
        mu = jnp.mean(r, axis=-1, keepdims=True)
        rc = r - mu
        var = jnp.mean(rc * rc, axis=-1, keepdims=True)
        o_ref[b] = rc * lax.rsqrt(var + LN_EPS) * lng_ref[...] + lnb_ref[...]
        return carry

    lax.fori_loop(0, BLOCK_B, phase3, 0, unroll=4)


def _full(shape):
    nd = len(shape)
    return pl.BlockSpec(shape, lambda i, j, _nd=nd: (0,) * _nd)


def _kv_project(mem, wkt16, wv16):
    bsz, m_len, d = mem.shape
    out_sds = jax.ShapeDtypeStruct((bsz, XA_WIDTH, m_len), BF16)
    v_sds = jax.ShapeDtypeStruct((bsz, m_len, XA_WIDTH), BF16)
    return pl.pallas_call(
        _kv_kernel,
        grid=(bsz,),
        in_specs=[pl.BlockSpec((1, m_len, d), lambda i: (i, 0, 0)),
                  pl.BlockSpec((XA_WIDTH, d), lambda i: (0, 0)),
                  pl.BlockSpec((d, XA_WIDTH), lambda i: (0, 0))],
        out_specs=[pl.BlockSpec((1, XA_WIDTH, m_len), lambda i: (i, 0, 0)),
                   pl.BlockSpec((1, m_len, XA_WIDTH), lambda i: (i, 0, 0))],
        out_shape=[out_sds, v_sds],
        name="kv_project",
    )(mem, wkt16, wv16)


def _layer(x, kt, vv, p, alpha):
    bsz, seq, d = x.shape
    m_len = kt.shape[2]
    grid = (bsz // BLOCK_B, seq // BLOCK_T)
    in_specs = [
        pl.BlockSpec((BLOCK_B, BLOCK_T, d), lambda i, j: (i, j, 0)),
        pl.BlockSpec((BLOCK_B, XA_WIDTH, m_len), lambda i, j: (i, 0, 0)),
        pl.BlockSpec((BLOCK_B, m_len, XA_WIDTH), lambda i, j: (i, 0, 0)),
        _full((d, IN_WIDTH)), _full((d, d)),
        _full((SSM_WIDTH, 2 * N_STATE)), _full((2 * N_STATE, SSM_WIDTH)),
        _full((SSM_WIDTH, SSM_WIDTH)),
        _full((GM_HEADS, CHUNK, CHUNK)), _full((GM_HEADS, CHUNK, GM_HEAD_DIM)),
        _full((1, GM_WIDTH)), _full((1, GM_WIDTH)),
        _full((1, N_STATE)), _full((1, N_STATE)), _full((1, SSM_WIDTH)),
        _full((1, SSM_WIDTH)), _full((1, d)), _full((1, d)),
    ]
    scratch = [
        pltpu.VMEM((ROWS, d), BF16),
        pltpu.VMEM((SSM_WIDTH // LANES, BLOCK_B * PERM_PITCH, LANES), F32),
        pltpu.VMEM((ROWS, SSM_WIDTH), F32),
        pltpu.VMEM((ROWS, 2 * N_STATE), F32),
        pltpu.VMEM((ROWS, SSM_WIDTH), F32),
        pltpu.VMEM((BLOCK_B, 2 * N_STATE), F32),
    ]
    return pl.pallas_call(
        functools.partial(_layer_kernel, alpha),
        grid=grid,
        in_specs=in_specs,
        out_specs=pl.BlockSpec((BLOCK_B, BLOCK_T, d), lambda i, j: (i, j, 0)),
        out_shape=jax.ShapeDtypeStruct(x.shape, x.dtype),
        scratch_shapes=scratch,
        compiler_params=pltpu.CompilerParams(
            dimension_semantics=("arbitrary", "arbitrary"),
            vmem_limit_bytes=VMEM_LIMIT_BYTES),
        name="trunk_layer",
    )(x, kt, vv, p["w_in"], p["w_out"], p["wb"], p["wc"], p["glu_w"],
      p["gm_w"], p["gm_bias"], p["gm_g"], p["gm_b"], p["lam_r"], p["lam_i"], p["d"],
      p["glu_b"], p["ln_g"], p["ln_b"])


def _ssm_discretize(lam_re, lam_im, log_step, b_re, b_im, c_re, c_im):
    step = jnp.exp(log_step)[:, None]
    mag = jnp.exp(lam_re * step)
    lbr = mag * jnp.cos(lam_im * step)
    lbi = mag * jnp.sin(lam_im * step)
    den = lam_re * lam_re + lam_im * lam_im
    fr = ((lbr - 1.0) * lam_re + lbi * lam_im) / den
    fi = (lbi * lam_re - (lbr - 1.0) * lam_im) / den
    bbr = fr[:, :, None] * b_re - fi[:, :, None] * b_im
    bbi = fr[:, :, None] * b_im + fi[:, :, None] * b_re
    eye = jnp.eye(SSM_GROUPS, dtype=F32)
    wb = jnp.concatenate(
        [jnp.einsum('gpc,gk->gckp', m, eye).reshape(SSM_WIDTH, N_STATE) for m in (bbr, bbi)],
        axis=1)
    wc = jnp.concatenate(
        [jnp.einsum('gcp,gk->gpkc', m, eye).reshape(N_STATE, SSM_WIDTH) for m in (c_re, -c_im)],
        axis=0)
    return lbr.reshape(1, N_STATE), lbi.reshape(1, N_STATE), wb, wc


def kernel(x, mem, w_in, gm_w_s, gm_b_s, gm_ln_g, gm_ln_b, ssm_lam_re, ssm_lam_im,
           ssm_log_step, ssm_b_re, ssm_b_im, ssm_c_re, ssm_c_im, ssm_d, glu_w, glu_b,
           xa_w_k, xa_w_v, w_out, ln_g, ln_b):
    depth = w_in.shape[0]
    alpha = float((2 * depth) ** 0.25)
    for l in range(depth):
        lam_r, lam_i, wb, wc = _ssm_discretize(
            ssm_lam_re[l], ssm_lam_im[l], ssm_log_step[l], ssm_b_re[l], ssm_b_im[l],
            ssm_c_re[l], ssm_c_im[l])
        p = {
            "w_in": w_in[l].astype(BF16), "w_out": w_out[l].astype(BF16),
            "wb": wb.astype(BF16), "wc": wc.astype(BF16), "glu_w": glu_w[l].astype(BF16),
            "gm_w": gm_w_s[l],
            "gm_bias": jnp.broadcast_to(gm_b_s[l][:, :, None], (GM_HEADS, CHUNK, GM_HEAD_DIM)),
            "gm_g": gm_ln_g[l].reshape(1, GM_WIDTH), "gm_b": gm_ln_b[l].reshape(1, GM_WIDTH),
            "lam_r": lam_r, "lam_i": lam_i, "d": ssm_d[l].reshape(1, SSM_WIDTH),
            "glu_b": glu_b[l].reshape(1, SSM_WIDTH),
            "ln_g": ln_g[l].reshape(1, D_MODEL), "ln_b": ln_b[l].reshape(1, D_MODEL),
        }
        kt, vv = _kv_project(mem, xa_w_k[l].T.astype(BF16), xa_w_v[l].astype(BF16))
        x = _layer(x, kt, vv, p, alpha)
    return x
```

```python
import functools
import math

import jax
import jax.numpy as jnp
from jax import lax
from jax.experimental import pallas as pl
from jax.experimental.pallas import tpu as pltpu

F32 = jnp.float32
BF16 = jnp.bfloat16

D_MODEL = 1024
GM_WIDTH = 512
GM_HEADS = 4
GM_HEAD_DIM = 128
CHUNK = 128
SSM_WIDTH = 256
SSM_GROUP = 16
SSM_GROUPS = 16
SSM_STATE = 64
N_STATE = SSM_GROUPS * SSM_STATE
XA_WIDTH = 256
XA_HEADS = 4
XA_HEAD_DIM = 64
IN_WIDTH = 3 * GM_WIDTH + 2 * SSM_WIDTH + 2 * XA_WIDTH
LN_EPS = 1e-5

C_U, C_V, C_GA = 0, GM_WIDTH, 2 * GM_WIDTH
C_XB = 3 * GM_WIDTH
C_GB = C_XB + SSM_WIDTH
C_Q = C_GB + SSM_WIDTH
C_GX = C_Q + XA_WIDTH
Y_A, Y_X, Y_B = 0, GM_WIDTH, GM_WIDTH + XA_WIDTH

BLOCK_B = 8
BLOCK_T = CHUNK
ROWS = BLOCK_B * BLOCK_T
LANES = 128
PERM_PITCH = BLOCK_T + 8
VMEM_LIMIT_BYTES = 60 * 1024 * 1024

_GELU_C = math.sqrt(2.0 / math.pi)


def _gelu(x):
    return 0.5 * x * (1.0 + jnp.tanh(_GELU_C * (x + 0.044715 * (x * x * x))))


def _sigmoid(x):
    return 1.0 / (1.0 + jnp.exp(-x))


def _silu(x):
    return x * _sigmoid(x)


def _dot(a, b):
    return jnp.dot(a, b, preferred_element_type=F32)


def _aligned(v, m):
    return v if isinstance(v, int) else pl.multiple_of(v, m)


def _kv_kernel(mem_ref, wkt_ref, wv_ref, kt_ref, v_ref):
    m16 = mem_ref[0].astype(BF16)
    kt = lax.dot_general(wkt_ref[...], m16, (((1,), (1,)), ((), ())),
                         preferred_element_type=F32)
    kt_ref[0] = kt.astype(BF16)
    v_ref[0] = _dot(m16, wv_ref[...]).astype(BF16)


def _layer_kernel(alpha,
                  x_ref, kt_ref, v_ref, w_in_ref, w_out_ref, wb_ref, wc_ref, gluw_ref,
                  gmw_ref, gmbias_ref, gmg_ref, gmb_ref, lamr_ref, lami_ref, d_ref,
                  glub_ref, lng_ref, lnb_ref,
                  o_ref,
                  z0_ref, z1_ref, wm_ref, y_ref, perm_ref, xbp_ref, bu_ref, gateb_ref, h_ref):
    j = pl.program_id(1)

    @pl.when(j == 0)
    def _():
        h_ref[...] = jnp.zeros_like(h_ref)

    tri = (lax.broadcasted_iota(jnp.int32, (CHUNK, CHUNK), 0)
           >= lax.broadcasted_iota(jnp.int32, (CHUNK, CHUNK), 1))
    for h in range(GM_HEADS):
        wm_ref[h] = jnp.where(tri, gmw_ref[h], 0.0).astype(BF16)
    row_head = lax.broadcasted_iota(jnp.int32, (XA_WIDTH, XA_WIDTH), 0) // XA_HEAD_DIM
    col_head = lax.broadcasted_iota(jnp.int32, (XA_WIDTH, XA_WIDTH), 1) // XA_HEAD_DIM
    out_head = lax.broadcasted_iota(jnp.int32, (BLOCK_T, XA_WIDTH), 1) // XA_HEAD_DIM

    def in_proj_fills(row, z_ref):
        x16 = x_ref[row].astype(BF16)

        def chunk_uv():
            z_ref[:, C_U:C_GA] = _dot(x16, w_in_ref[:, C_U:C_GA])

        def chunk_ssm():
            zz = _dot(x16, w_in_ref[:, C_GA:C_Q])
            z_ref[:, C_GA:C_XB] = zz[:, 0:GM_WIDTH]
            xs = zz[:, C_XB - C_GA:C_GB - C_GA]
            p0 = _aligned(row * PERM_PITCH, 8)
            for s in range(SSM_WIDTH // LANES):
                perm_ref[s, pl.ds(p0, BLOCK_T), :] = xs[:, s * LANES:(s + 1) * LANES]
            r0 = _aligned(row * BLOCK_T, BLOCK_T)
            gateb_ref[pl.ds(r0, BLOCK_T), :] = _silu(zz[:, C_GB - C_GA:])

        def chunk_attn():
            z_ref[:, C_Q:] = _dot(x16, w_in_ref[:, C_Q:])

        return chunk_uv, chunk_ssm, chunk_attn

    def branches(row, z_ref, fills):
        r0 = _aligned(row * BLOCK_T, BLOCK_T)
        q16 = (z_ref[:, C_Q:C_GX] * (XA_HEAD_DIM ** -0.5)).astype(BF16)
        kt = kt_ref[row]
        vv = v_ref[row]
        scores = [_dot(q16, jnp.where(row_head == h, kt, jnp.zeros_like(kt)))
                  for h in range(XA_HEADS)]
        fills[0]()
        o = None
        scale = None
        for h in range(XA_HEADS):
            s = scores[h]
            p = jnp.exp(s - jnp.max(s, axis=-1, keepdims=True))
            inv = 1.0 / jnp.sum(p, axis=-1, keepdims=True)
            oh = _dot(p.astype(BF16), jnp.where(col_head == h, vv, jnp.zeros_like(vv)))
            o = oh if o is None else o + oh
            scale = inv if scale is None else jnp.where(out_head == h, inv, scale)
        v = _gelu(z_ref[:, C_V:C_GA])
        mixed = []
        for h in range(GM_HEADS):
            sl = slice(h * GM_HEAD_DIM, (h + 1) * GM_HEAD_DIM)
            vh = v[:, sl]
            mu = jnp.mean(vh, axis=-1, keepdims=True)
            xc = vh - mu
            var = jnp.mean(xc * xc, axis=-1, keepdims=True)
            vn = xc * lax.rsqrt(var + LN_EPS) * gmg_ref[:, sl] + gmb_ref[:, sl]
            mixed.append(_dot(wm_ref[h], vn.astype(BF16)) + gmbias_ref[h])
        fills[1]()
        fills[2]()
        u = _gelu(z_ref[:, C_U:C_V])
        ya = u * jnp.concatenate(mixed, axis=1) * _silu(z_ref[:, C_GA:C_XB])
        y_ref[pl.ds(r0, BLOCK_T), Y_A:Y_X] = ya.astype(BF16)
        yx = o * scale * _silu(z_ref[:, C_GX:])
        y_ref[pl.ds(r0, BLOCK_T), Y_X:Y_B] = yx.astype(BF16)

    for f in in_proj_fills(0, z0_ref):
        f()

    def row_pair(kk, carry):
        k = 2 * kk
        branches(k, z0_ref, in_proj_fills(k + 1, z1_ref))
        branches(k + 1, z1_ref, in_proj_fills(k + 2, z0_ref))
        return carry

    lax.fori_loop(0, (BLOCK_B - 2) // 2, row_pair, 0)
    branches(BLOCK_B - 2, z0_ref, in_proj_fills(BLOCK_B - 1, z1_ref))

    n_slab = SSM_WIDTH // LANES

    def to_time_major(t, carry):
        r0 = pl.multiple_of(t * BLOCK_B, BLOCK_B)
        for s in range(n_slab):
            xbp_ref[pl.ds(r0, BLOCK_B), s * LANES:(s + 1) * LANES] = (
                perm_ref[s, pl.ds(t, BLOCK_B, stride=PERM_PITCH), :])
        return carry

    lax.fori_loop(0, BLOCK_T, to_time_major, 0, unroll=8)

    half = ROWS // 2

    def b_proj(rows):
        def f():
            bu_ref[rows, :] = _dot(xbp_ref[rows, :].astype(BF16), wb_ref[...])
        return f

    branches(BLOCK_B - 1, z1_ref,
             (b_proj(slice(0, half)), b_proj(slice(half, ROWS)), lambda: None))

    lam_r = jnp.broadcast_to(lamr_ref[...], (BLOCK_B, N_STATE))
    lam_i = jnp.broadcast_to(lami_ref[...], (BLOCK_B, N_STATE))

    def scan_step(t, carry):
        hr, hi = carry
        r0 = pl.multiple_of(t * BLOCK_B, BLOCK_B)
        br = bu_ref[pl.ds(r0, BLOCK_B), 0:N_STATE]
        bi = bu_ref[pl.ds(r0, BLOCK_B), N_STATE:]
        hr, hi = lam_r * hr - lam_i * hi + br, lam_r * hi + lam_i * hr + bi
        bu_ref[pl.ds(r0, BLOCK_B), 0:N_STATE] = hr
        bu_ref[pl.ds(r0, BLOCK_B), N_STATE:] = hi
        return hr, hi

    hr, hi = lax.fori_loop(0, BLOCK_T, scan_step,
                           (h_ref[:, 0:N_STATE], h_ref[:, N_STATE:]), unroll=4)
    h_ref[:, 0:N_STATE] = hr
    h_ref[:, N_STATE:] = hi

    xbp = xbp_ref[...]
    ys = _dot(bu_ref[...].astype(BF16), wc_ref[...]) + d_ref[...] * xbp
    yg = _gelu(ys)
    yb = yg * _sigmoid(_dot(yg.astype(BF16), gluw_ref[...]) + glub_ref[...])
    xbp_ref[...] = yb

    def to_batch_major(t, carry):
        r0 = pl.multiple_of(t * BLOCK_B, BLOCK_B)
        for s in range(n_slab):
            perm_ref[s, pl.ds(t, BLOCK_B, stride=PERM_PITCH), :] = (
                xbp_ref[pl.ds(r0, BLOCK_B), s * LANES:(s + 1) * LANES])
        return carry

    lax.fori_loop(0, BLOCK_T, to_batch_major, 0, unroll=8)

    for b in range(BLOCK_B):
        ybb = jnp.concatenate(
            [perm_ref[s, b * PERM_PITCH:b * PERM_PITCH + BLOCK_T, :] for s in range(n_slab)],
            axis=1)
        rows = slice(b * BLOCK_T, (b + 1) * BLOCK_T)
        y_ref[rows, Y_B:] = (ybb * gateb_ref[rows, :]).astype(BF16)

    def phase3(b, carry):
        r0 = pl.multiple_of(b * BLOCK_T, BLOCK_T)
        r = alpha * x_ref[b] + _dot(y_ref[pl.ds(r0, BLOCK_T), :], w_out_ref[...])
        mu = jnp.mean(r, axis=-1, keepdims=True)
        rc = r - mu
        var = jnp.mean(rc * rc, axis=-1, keepdims=True)
        o_ref[b] = rc * lax.rsqrt(var + LN_EPS) * lng_ref[...] + lnb_ref[...]
        return carry

    lax.fori_loop(0, BLOCK_B, phase3, 0, unroll=4)


def _full(shape):
    nd = len(shape)
    return pl.BlockSpec(shape, lambda i, j, _nd=nd: (0,) * _nd)


def _kv_project(mem, wkt16, wv16):
    bsz, m_len, d = mem.shape
    out_sds = jax.ShapeDtypeStruct((bsz, XA_WIDTH, m_len), BF16)
    v_sds = jax.ShapeDtypeStruct((bsz, m_len, XA_WIDTH), BF16)
    return pl.pallas_call(
        _kv_kernel,
        grid=(bsz,),
        in_specs=[pl.BlockSpec((1, m_len, d), lambda i: (i, 0, 0)),
                  pl.BlockSpec((XA_WIDTH, d), lambda i: (0, 0)),
                  pl.BlockSpec((d, XA_WIDTH), lambda i: (0, 0))],
        out_specs=[pl.BlockSpec((1, XA_WIDTH, m_len), lambda i: (i, 0, 0)),
                   pl.BlockSpec((1, m_len, XA_WIDTH), lambda i: (i, 0, 0))],
        out_shape=[out_sds, v_sds],
        name="kv_project",
    )(mem, wkt16, wv16)


def _layer(x, kt, vv, p, alpha):
    bsz, seq, d = x.shape
    m_len = kt.shape[2]
    grid = (bsz // BLOCK_B, seq // BLOCK_T)
    in_specs = [
        pl.BlockSpec((BLOCK_B, BLOCK_T, d), lambda i, j: (i, j, 0)),
        pl.BlockSpec((BLOCK_B, XA_WIDTH, m_len), lambda i, j: (i, 0, 0)),
        pl.BlockSpec((BLOCK_B, m_len, XA_WIDTH), lambda i, j: (i, 0, 0)),
        _full((d, IN_WIDTH)), _full((d, d)),
        _full((SSM_WIDTH, 2 * N_STATE)), _full((2 * N_STATE, SSM_WIDTH)),
        _full((SSM_WIDTH, SSM_WIDTH)),
        _full((GM_HEADS, CHUNK, CHUNK)), _full((GM_HEADS, CHUNK, GM_HEAD_DIM)),
        _full((1, GM_WIDTH)), _full((1, GM_WIDTH)),
        _full((1, N_STATE)), _full((1, N_STATE)), _full((1, SSM_WIDTH)),
        _full((1, SSM_WIDTH)), _full((1, d)), _full((1, d)),
    ]
    scratch = [
        pltpu.VMEM((BLOCK_T, IN_WIDTH), F32),
        pltpu.VMEM((BLOCK_T, IN_WIDTH), F32),
        pltpu.VMEM((GM_HEADS, CHUNK, CHUNK), BF16),
        pltpu.VMEM((ROWS, d), BF16),
        pltpu.VMEM((SSM_WIDTH // LANES, BLOCK_B * PERM_PITCH, LANES), F32),
        pltpu.VMEM((ROWS, SSM_WIDTH), F32),
        pltpu.VMEM((ROWS, 2 * N_STATE), F32),
        pltpu.VMEM((ROWS, SSM_WIDTH), F32),
        pltpu.VMEM((BLOCK_B, 2 * N_STATE), F32),
    ]
    return pl.pallas_call(
        functools.partial(_layer_kernel, alpha),
        grid=grid,
        in_specs=in_specs,
        out_specs=pl.BlockSpec((BLOCK_B, BLOCK_T, d), lambda i, j: (i, j, 0)),
        out_shape=jax.ShapeDtypeStruct(x.shape, x.dtype),
        scratch_shapes=scratch,
        compiler_params=pltpu.CompilerParams(
            dimension_semantics=("arbitrary", "arbitrary"),
            vmem_limit_bytes=VMEM_LIMIT_BYTES),
        name="trunk_layer",
    )(x, kt, vv, p["w_in"], p["w_out"], p["wb"], p["wc"], p["glu_w"],
      p["gm_w"], p["gm_bias"], p["gm_g"], p["gm_b"], p["lam_r"], p["lam_i"], p["d"],
      p["glu_b"], p["ln_g"], p["ln_b"])


def _ssm_discretize(lam_re, lam_im, log_step, b_re, b_im, c_re, c_im):
    step = jnp.exp(log_step)[:, None]
    mag = jnp.exp(lam_re * step)
    lbr = mag * jnp.cos(lam_im * step)
    lbi = mag * jnp.sin(lam_im * step)
    den = lam_re * lam_re + lam_im * lam_im
    fr = ((lbr - 1.0) * lam_re + lbi * lam_im) / den
    fi = (lbi * lam_re - (lbr - 1.0) * lam_im) / den
    bbr = fr[:, :, None] * b_re - fi[:, :, None] * b_im
    bbi = fr[:, :, None] * b_im + fi[:, :, None] * b_re
    eye = jnp.eye(SSM_GROUPS, dtype=F32)
    wb = jnp.concatenate(
        [jnp.einsum('gpc,gk->gckp', m, eye).reshape(SSM_WIDTH, N_STATE) for m in (bbr, bbi)],
        axis=1)
    wc = jnp.concatenate(
        [jnp.einsum('gcp,gk->gpkc', m, eye).reshape(N_STATE, SSM_WIDTH) for m in (c_re, -c_im)],
        axis=0)
    return lbr.reshape(1, N_STATE), lbi.reshape(1, N_STATE), wb, wc


def kernel(x, mem, w_in, gm_w_s, gm_b_s, gm_ln_g, gm_ln_b, ssm_lam_re, ssm_lam_im,
           ssm_log_step, ssm_b_re, ssm_b_im, ssm_c_re, ssm_c_im, ssm_d, glu_w, glu_b,
           xa_w_k, xa_w_v, w_out, ln_g, ln_b):
    depth = w_in.shape[0]
    alpha = float((2 * depth) ** 0.25)
    a_end = GM_WIDTH
    b_end = GM_WIDTH + SSM_WIDTH
    for l in range(depth):
        lam_r, lam_i, wb, wc = _ssm_discretize(
            ssm_lam_re[l], ssm_lam_im[l], ssm_log_step[l], ssm_b_re[l], ssm_b_im[l],
            ssm_c_re[l], ssm_c_im[l])
        w_out_axb = jnp.concatenate(
            [w_out[l, :a_end], w_out[l, b_end:], w_out[l, a_end:b_end]], axis=0)
        p = {
            "w_in": w_in[l].astype(BF16), "w_out": w_out_axb.astype(BF16),
            "wb": wb.astype(BF16), "wc": wc.astype(BF16), "glu_w": glu_w[l].astype(BF16),
            "gm_w": gm_w_s[l],
            "gm_bias": jnp.broadcast_to(gm_b_s[l][:, :, None], (GM_HEADS, CHUNK, GM_HEAD_DIM)),
            "gm_g": gm_ln_g[l].reshape(1, GM_WIDTH), "gm_b": gm_ln_b[l].reshape(1, GM_WIDTH),
            "lam_r": lam_r, "lam_i": lam_i, "d": ssm_d[l].reshape(1, SSM_WIDTH),
            "glu_b": glu_b[l].reshape(1, SSM_WIDTH),
            "ln_g": ln_g[l].reshape(1, D_MODEL), "ln_b": ln_b[l].reshape(1, D_MODEL),
        }
        kt, vv = _kv_project(mem, xa_w_k[l].T.astype(BF16), xa_w_v[l].astype(BF16))
        x = _layer(x, kt, vv, p, alpha)
    return x
```

```python
import functools
import math

import jax
import jax.numpy as jnp
from jax import lax
from jax.experimental import pallas as pl
from jax.experimental.pallas import tpu as pltpu

F32 = jnp.float32
BF16 = jnp.bfloat16

D_MODEL = 1024
GM_WIDTH = 512
GM_HEADS = 4
GM_HEAD_DIM = 128
CHUNK = 128
SSM_WIDTH = 256
SSM_GROUP = 16
SSM_GROUPS = 16
SSM_STATE = 64
N_STATE = SSM_GROUPS * SSM_STATE
XA_WIDTH = 256
XA_HEADS = 4
XA_HEAD_DIM = 64
IN_WIDTH = 3 * GM_WIDTH + 2 * SSM_WIDTH + 2 * XA_WIDTH
LN_EPS = 1e-5

C_U, C_V, C_GA = 0, GM_WIDTH, 2 * GM_WIDTH
C_XB = 3 * GM_WIDTH
C_GB = C_XB + SSM_WIDTH
C_Q = C_GB + SSM_WIDTH
C_GX = C_Q + XA_WIDTH
Y_A, Y_X, Y_B = 0, GM_WIDTH, GM_WIDTH + XA_WIDTH

BLOCK_B = 8
BLOCK_T = CHUNK
ROWS = BLOCK_B * BLOCK_T
S5_CHUNKS = 4
LANES = 128
PERM_PITCH = BLOCK_T + 8
VMEM_LIMIT_BYTES = 60 * 1024 * 1024

_GELU_C = math.sqrt(2.0 / math.pi)


def _gelu(x):
    return 0.5 * x * (1.0 + jnp.tanh(_GELU_C * (x + 0.044715 * (x * x * x))))


def _sigmoid(x):
    return 1.0 / (1.0 + jnp.exp(-x))


def _silu(x):
    return x * _sigmoid(x)


def _dot(a, b):
    return jnp.dot(a, b, preferred_element_type=F32)


def _aligned(v, m):
    return v if isinstance(v, int) else pl.multiple_of(v, m)


def _kv_kernel(mem_ref, wkt_ref, wv_ref, kt_ref, v_ref):
    m16 = mem_ref[0].astype(BF16)
    kt = lax.dot_general(wkt_ref[...], m16, (((1,), (1,)), ((), ())),
                         preferred_element_type=F32)
    kt_ref[0] = kt.astype(BF16)
    v_ref[0] = _dot(m16, wv_ref[...]).astype(BF16)


def _layer_kernel(alpha,
                  x_ref, kt_ref, v_ref, w_in_ref, w_out_ref, wb_ref, wc_ref, gluw_ref,
                  gmw_ref, gmbias_ref, gmg_ref, gmb_ref, lamr_ref, lami_ref, d_ref,
                  glub_ref, lng_ref, lnb_ref,
                  o_ref,
                  z0_ref, z1_ref, wm_ref, y_ref, perm_ref, xbp_ref, bu_ref, gateb_ref, h_ref):
    j = pl.program_id(1)

    @pl.when(j == 0)
    def _():
        h_ref[...] = jnp.zeros_like(h_ref)

    tri = (lax.broadcasted_iota(jnp.int32, (CHUNK, CHUNK), 0)
           >= lax.broadcasted_iota(jnp.int32, (CHUNK, CHUNK), 1))
    for h in range(GM_HEADS):
        wm_ref[h] = jnp.where(tri, gmw_ref[h], 0.0).astype(BF16)
    row_head = lax.broadcasted_iota(jnp.int32, (XA_WIDTH, XA_WIDTH), 0) // XA_HEAD_DIM
    col_head = lax.broadcasted_iota(jnp.int32, (XA_WIDTH, XA_WIDTH), 1) // XA_HEAD_DIM
    out_head = lax.broadcasted_iota(jnp.int32, (BLOCK_T, XA_WIDTH), 1) // XA_HEAD_DIM

    def in_proj_fills(row, z_ref):
        x16 = x_ref[row].astype(BF16)

        def chunk_uv():
            z_ref[:, C_U:C_GA] = _dot(x16, w_in_ref[:, C_U:C_GA])

        def chunk_ssm():
            zz = _dot(x16, w_in_ref[:, C_GA:C_Q])
            z_ref[:, C_GA:C_XB] = zz[:, 0:GM_WIDTH]
            xs = zz[:, C_XB - C_GA:C_GB - C_GA]
            p0 = _aligned(row * PERM_PITCH, 8)
            for s in range(SSM_WIDTH // LANES):
                perm_ref[s, pl.ds(p0, BLOCK_T), :] = xs[:, s * LANES:(s + 1) * LANES]
            r0 = _aligned(row * BLOCK_T, BLOCK_T)
            gateb_ref[pl.ds(r0, BLOCK_T), :] = _silu(zz[:, C_GB - C_GA:])

        def chunk_attn():
            z_ref[:, C_Q:] = _dot(x16, w_in_ref[:, C_Q:])

        return chunk_uv, chunk_ssm, chunk_attn

    def branches(row, z_ref, fills):
        r0 = _aligned(row * BLOCK_T, BLOCK_T)
        q16 = (z_ref[:, C_Q:C_GX] * (XA_HEAD_DIM ** -0.5)).astype(BF16)
        kt = kt_ref[row]
        vv = v_ref[row]
        scores = [_dot(q16, jnp.where(row_head == h, kt, jnp.zeros_like(kt)))
                  for h in range(XA_HEADS)]
        fills[0]()
        o = None
        scale = None
        for h in range(XA_HEADS):
            s = scores[h]
            p = jnp.exp(s - jnp.max(s, axis=-1, keepdims=True))
            inv = 1.0 / jnp.sum(p, axis=-1, keepdims=True)
            oh = _dot(p.astype(BF16), jnp.where(col_head == h, vv, jnp.zeros_like(vv)))
            o = oh if o is None else o + oh
            scale = inv if scale is None else jnp.where(out_head == h, inv, scale)
        v = _gelu(z_ref[:, C_V:C_GA])
        mixed = []
        for h in range(GM_HEADS):
            sl = slice(h * GM_HEAD_DIM, (h + 1) * GM_HEAD_DIM)
            vh = v[:, sl]
            mu = jnp.mean(vh, axis=-1, keepdims=True)
            xc = vh - mu
            var = jnp.mean(xc * xc, axis=-1, keepdims=True)
            vn = xc * lax.rsqrt(var + LN_EPS) * gmg_ref[:, sl] + gmb_ref[:, sl]
            mixed.append(_dot(wm_ref[h], vn.astype(BF16)) + gmbias_ref[h])
        fills[1]()
        fills[2]()
        u = _gelu(z_ref[:, C_U:C_V])
        ya = u * jnp.concatenate(mixed, axis=1) * _silu(z_ref[:, C_GA:C_XB])
        y_ref[pl.ds(r0, BLOCK_T), Y_A:Y_X] = ya.astype(BF16)
        yx = o * scale * _silu(z_ref[:, C_GX:])
        y_ref[pl.ds(r0, BLOCK_T), Y_X:Y_B] = yx.astype(BF16)

    for f in in_proj_fills(0, z0_ref):
        f()

    def row_pair(kk, carry):
        k = 2 * kk
        branches(k, z0_ref, in_proj_fills(k + 1, z1_ref))
        branches(k + 1, z1_ref, in_proj_fills(k + 2, z0_ref))
        return carry

    lax.fori_loop(0, (BLOCK_B - 2) // 2, row_pair, 0)
    branches(BLOCK_B - 2, z0_ref, in_proj_fills(BLOCK_B - 1, z1_ref))

    n_slab = SSM_WIDTH // LANES

    def to_time_major(t, carry):
        r0 = pl.multiple_of(t * BLOCK_B, BLOCK_B)
        for s in range(n_slab):
            xbp_ref[pl.ds(r0, BLOCK_B), s * LANES:(s + 1) * LANES] = (
                perm_ref[s, pl.ds(t, BLOCK_B, stride=PERM_PITCH), :])
        return carry

    lax.fori_loop(0, BLOCK_T, to_time_major, 0, unroll=8)

    half = ROWS // 2

    def b_proj(rows):
        def f():
            bu_ref[rows, :] = _dot(xbp_ref[rows, :].astype(BF16), wb_ref[...])
        return f

    branches(BLOCK_B - 1, z1_ref,
             (b_proj(slice(0, half)), b_proj(slice(half, ROWS)), lambda: None))

    lam_r = jnp.broadcast_to(lamr_ref[...], (BLOCK_B, N_STATE))
    lam_i = jnp.broadcast_to(lami_ref[...], (BLOCK_B, N_STATE))

    steps = BLOCK_T // S5_CHUNKS
    state = [h_ref[:, 0:N_STATE], h_ref[:, N_STATE:]]

    def scan_chunk(c):
        hr, hi = state
        for t in range(c * steps, (c + 1) * steps):
            rows = slice(t * BLOCK_B, (t + 1) * BLOCK_B)
            br = bu_ref[rows, 0:N_STATE]
            bi = bu_ref[rows, N_STATE:]
            hr, hi = lam_r * hr - lam_i * hi + br, lam_r * hi + lam_i * hr + bi
            bu_ref[rows, 0:N_STATE] = hr
            bu_ref[rows, N_STATE:] = hi
        state[0], state[1] = hr, hi

    def chunk_rows(c):
        return slice(c * steps * BLOCK_B, (c + 1) * steps * BLOCK_B)

    def c_proj(c):
        rows = chunk_rows(c)
        return _dot(bu_ref[rows, :].astype(BF16), wc_ref[...]) + d_ref[...] * xbp_ref[rows, :]

    def glu(c, ys):
        yg = _gelu(ys)
        xbp_ref[chunk_rows(c), :] = (
            yg * _sigmoid(_dot(yg.astype(BF16), gluw_ref[...]) + glub_ref[...]))

    ys_prev = None
    for c in range(S5_CHUNKS):
        scan_chunk(c)
        ys_c = c_proj(c)
        if ys_prev is not None:
            glu(c - 1, ys_prev)
        ys_prev = ys_c
    glu(S5_CHUNKS - 1, ys_prev)
    h_ref[:, 0:N_STATE] = state[0]
    h_ref[:, N_STATE:] = state[1]

    def to_batch_major(t, carry):
        r0 = pl.multiple_of(t * BLOCK_B, BLOCK_B)
        for s in range(n_slab):
            perm_ref[s, pl.ds(t, BLOCK_B, stride=PERM_PITCH), :] = (
                xbp_ref[pl.ds(r0, BLOCK_B), s * LANES:(s + 1) * LANES])
        return carry

    lax.fori_loop(0, BLOCK_T, to_batch_major, 0, unroll=8)

    for b in range(BLOCK_B):
        ybb = jnp.concatenate(
            [perm_ref[s, b * PERM_PITCH:b * PERM_PITCH + BLOCK_T, :] for s in range(n_slab)],
            axis=1)
        rows = slice(b * BLOCK_T, (b + 1) * BLOCK_T)
        y_ref[rows, Y_B:] = (ybb * gateb_ref[rows, :]).astype(BF16)

    def phase3(b, carry):
        r0 = pl.multiple_of(b * BLOCK_T, BLOCK_T)
        r = alpha * x_ref[b] + _dot(y_ref[pl.ds(r0, BLOCK_T), :], w_out_ref[...])
        mu = jnp.mean(r, axis=-1, keepdims=True)
        rc = r - mu
        var = jnp.mean(rc * rc, axis=-1, keepdims=True)
        o_ref[b] = rc * lax.rsqrt(var + LN_EPS) * lng_ref[...] + lnb_ref[...]
        return carry

    lax.fori_loop(0, BLOCK_B, phase3, 0, unroll=4)


def _full(shape):
    nd = len(shape)
    return pl.BlockSpec(shape, lambda i, j, _nd=nd: (0,) * _nd)


def _kv_project(mem, wkt16, wv16):
    bsz, m_len, d = mem.shape
    out_sds = jax.ShapeDtypeStruct((bsz, XA_WIDTH, m_len), BF16)
    v_sds = jax.ShapeDtypeStruct((bsz, m_len, XA_WIDTH), BF16)
    return pl.pallas_call(
        _kv_kernel,
        grid=(bsz,),
        in_specs=[pl.BlockSpec((1, m_len, d), lambda i: (i, 0, 0)),
                  pl.BlockSpec((XA_WIDTH, d), lambda i: (0, 0)),
                  pl.BlockSpec((d, XA_WIDTH), lambda i: (0, 0))],
        out_specs=[pl.BlockSpec((1, XA_WIDTH, m_len), lambda i: (i, 0, 0)),
                   pl.BlockSpec((1, m_len, XA_WIDTH), lambda i: (i, 0, 0))],
        out_shape=[out_sds, v_sds],
        name="kv_project",
    )(mem, wkt16, wv16)


def _layer(x, kt, vv, p, alpha):
    bsz, seq, d = x.shape
    m_len = kt.shape[2]
    grid = (bsz // BLOCK_B, seq // BLOCK_T)
    in_specs = [
        pl.BlockSpec((BLOCK_B, BLOCK_T, d), lambda i, j: (i, j, 0)),
        pl.BlockSpec((BLOCK_B, XA_WIDTH, m_len), lambda i, j: (i, 0, 0)),
        pl.BlockSpec((BLOCK_B, m_len, XA_WIDTH), lambda i, j: (i, 0, 0)),
        _full((d, IN_WIDTH)), _full((d, d)),
        _full((SSM_WIDTH, 2 * N_STATE)), _full((2 * N_STATE, SSM_WIDTH)),
        _full((SSM_WIDTH, SSM_WIDTH)),
        _full((GM_HEADS, CHUNK, CHUNK)), _full((GM_HEADS, CHUNK, GM_HEAD_DIM)),
        _full((1, GM_WIDTH)), _full((1, GM_WIDTH)),
        _full((1, N_STATE)), _full((1, N_STATE)), _full((1, SSM_WIDTH)),
        _full((1, SSM_WIDTH)), _full((1, d)), _full((1, d)),
    ]
    scratch = [
        pltpu.VMEM((BLOCK_T, IN_WIDTH), F32),
        pltpu.VMEM((BLOCK_T, IN_WIDTH), F32),
        pltpu.VMEM((GM_HEADS, CHUNK, CHUNK), BF16),
        pltpu.VMEM((ROWS, d), BF16),
        pltpu.VMEM((SSM_WIDTH // LANES, BLOCK_B * PERM_PITCH, LANES), F32),
        pltpu.VMEM((ROWS, SSM_WIDTH), F32),
        pltpu.VMEM((ROWS, 2 * N_STATE), F32),
        pltpu.VMEM((ROWS, SSM_WIDTH), F32),
        pltpu.VMEM((BLOCK_B, 2 * N_STATE), F32),
    ]
    return pl.pallas_call(
        functools.partial(_layer_kernel, alpha),
        grid=grid,
        in_specs=in_specs,
        out_specs=pl.BlockSpec((BLOCK_B, BLOCK_T, d), lambda i, j: (i, j, 0)),
        out_shape=jax.ShapeDtypeStruct(x.shape, x.dtype),
        scratch_shapes=scratch,
        compiler_params=pltpu.CompilerParams(
            dimension_semantics=("arbitrary", "arbitrary"),
            vmem_limit_bytes=VMEM_LIMIT_BYTES),
        name="trunk_layer",
    )(x, kt, vv, p["w_in"], p["w_out"], p["wb"], p["wc"], p["glu_w"],
      p["gm_w"], p["gm_bias"], p["gm_g"], p["gm_b"], p["lam_r"], p["lam_i"], p["d"],
      p["glu_b"], p["ln_g"], p["ln_b"])


def _ssm_discretize(lam_re, lam_im, log_step, b_re, b_im, c_re, c_im):
    step = jnp.exp(log_step)[:, None]
    mag = jnp.exp(lam_re * step)
    lbr = mag * jnp.cos(lam_im * step)
    lbi = mag * jnp.sin(lam_im * step)
    den = lam_re * lam_re + lam_im * lam_im
    fr = ((lbr - 1.0) * lam_re + lbi * lam_im) / den
    fi = (lbi * lam_re - (lbr - 1.0) * lam_im) / den
    bbr = fr[:, :, None] * b_re - fi[:, :, None] * b_im
    bbi = fr[:, :, None] * b_im + fi[:, :, None] * b_re
    eye = jnp.eye(SSM_GROUPS, dtype=F32)
    wb = jnp.concatenate(
        [jnp.einsum('gpc,gk->gckp', m, eye).reshape(SSM_WIDTH, N_STATE) for m in (bbr, bbi)],
        axis=1)
    wc = jnp.concatenate(
        [jnp.einsum('gcp,gk->gpkc', m, eye).reshape(N_STATE, SSM_WIDTH) for m in (c_re, -c_im)],
        axis=0)
    return lbr.reshape(1, N_STATE), lbi.reshape(1, N_STATE), wb, wc


def kernel(x, mem, w_in, gm_w_s, gm_b_s, gm_ln_g, gm_ln_b, ssm_lam_re, ssm_lam_im,
           ssm_log_step, ssm_b_re, ssm_b_im, ssm_c_re, ssm_c_im, ssm_d, glu_w, glu_b,
           xa_w_k, xa_w_v, w_out, ln_g, ln_b):
    depth = w_in.shape[0]
    alpha = float((2 * depth) ** 0.25)
    a_end = GM_WIDTH
    b_end = GM_WIDTH + SSM_WIDTH
    for l in range(depth):
        lam_r, lam_i, wb, wc = _ssm_discretize(
            ssm_lam_re[l], ssm_lam_im[l], ssm_log_step[l], ssm_b_re[l], ssm_b_im[l],
            ssm_c_re[l], ssm_c_im[l])
        w_out_axb = jnp.concatenate(
            [w_out[l, :a_end], w_out[l, b_end:], w_out[l, a_end:b_end]], axis=0)
        p = {
            "w_in": w_in[l].astype(BF16), "w_out": w_out_axb.astype(BF16),
            "wb": wb.astype(BF16), "wc": wc.astype(BF16), "glu_w": glu_w[l].astype(BF16),
            "gm_w": gm_w_s[l],
            "gm_bias": jnp.broadcast_to(gm_b_s[l][:, :, None], (GM_HEADS, CHUNK, GM_HEAD_DIM)),
            "gm_g": gm_ln_g[l].reshape(1, GM_WIDTH), "gm_b": gm_ln_b[l].reshape(1, GM_WIDTH),
            "lam_r": lam_r, "lam_i": lam_i, "d": ssm_d[l].reshape(1, SSM_WIDTH),
            "glu_b": glu_b[l].reshape(1, SSM_WIDTH),
            "ln_g": ln_g[l].reshape(1, D_MODEL), "ln_b": ln_b[l].reshape(1, D_MODEL),
        }
        kt, vv = _kv_project(mem, xa_w_k[l].T.astype(BF16), xa_w_v[l].astype(BF16))
        x = _layer(x, kt, vv, p, alpha)
    return x
```

```python
import functools
import math

import jax
import jax.numpy as jnp
from jax import lax
from jax.experimental import pallas as pl
from jax.experimental.pallas import tpu as pltpu

F32 = jnp.float32
BF16 = jnp.bfloat16

D_MODEL = 1024
GM_WIDTH = 512
GM_HEADS = 4
GM_HEAD_DIM = 128
CHUNK = 128
SSM_WIDTH = 256
SSM_GROUP = 16
SSM_GROUPS = 16
SSM_STATE = 64
N_STATE = SSM_GROUPS * SSM_STATE
XA_WIDTH = 256
XA_HEADS = 4
XA_HEAD_DIM = 64
IN_WIDTH = 3 * GM_WIDTH + 2 * SSM_WIDTH + 2 * XA_WIDTH
LN_EPS = 1e-5

C_U, C_V, C_GA = 0, GM_WIDTH, 2 * GM_WIDTH
C_XB = 3 * GM_WIDTH
C_GB = C_XB + SSM_WIDTH
C_Q = C_GB + SSM_WIDTH
C_GX = C_Q + XA_WIDTH
Y_A, Y_X, Y_B = 0, GM_WIDTH, GM_WIDTH + XA_WIDTH

BLOCK_B = 8
BLOCK_T = CHUNK
ROWS = BLOCK_B * BLOCK_T
S5_CHUNKS = 4
LANES = 128
PERM_PITCH = BLOCK_T + 8
VMEM_LIMIT_BYTES = 60 * 1024 * 1024

_GELU_C = math.sqrt(2.0 / math.pi)


def _gelu(x):
    return 0.5 * x * (1.0 + jnp.tanh(_GELU_C * (x + 0.044715 * (x * x * x))))


def _sigmoid(x):
    return 1.0 / (1.0 + jnp.exp(-x))


def _silu(x):
    return x * _sigmoid(x)


def _dot(a, b):
    return jnp.dot(a, b, preferred_element_type=F32)


def _aligned(v, m):
    return v if isinstance(v, int) else pl.multiple_of(v, m)


def _kv_kernel(mem_ref, wkt_ref, wv_ref, kt_ref, v_ref):
    m16 = mem_ref[0].astype(BF16)
    kt = lax.dot_general(wkt_ref[...], m16, (((1,), (1,)), ((), ())),
                         preferred_element_type=F32)
    kt_ref[0] = kt.astype(BF16)
    v_ref[0] = _dot(m16, wv_ref[...]).astype(BF16)


def _layer_kernel(alpha,
                  x_ref, kt_ref, v_ref, w_in_ref, w_out_ref, wb_ref, wc_ref, gluw_ref,
                  gmw_ref, gmbias_ref, gmg_ref, gmb_ref, lamr_ref, lami_ref, d_ref,
                  glub_ref, lng_ref, lnb_ref,
                  o_ref,
                  z0_ref, z1_ref, wm_ref, y_ref, perm_ref, xbp_ref, bu_ref, gateb_ref, h_ref):
    j = pl.program_id(1)

    @pl.when(j == 0)
    def _():
        h_ref[...] = jnp.zeros_like(h_ref)

    tri = (lax.broadcasted_iota(jnp.int32, (CHUNK, CHUNK), 0)
           >= lax.broadcasted_iota(jnp.int32, (CHUNK, CHUNK), 1))
    for h in range(GM_HEADS):
        wm_ref[h] = jnp.where(tri, gmw_ref[h], 0.0).astype(BF16)
    row_head = lax.broadcasted_iota(jnp.int32, (XA_WIDTH, XA_WIDTH), 0) // XA_HEAD_DIM
    col_head = lax.broadcasted_iota(jnp.int32, (XA_WIDTH, XA_WIDTH), 1) // XA_HEAD_DIM
    out_head = lax.broadcasted_iota(jnp.int32, (BLOCK_T, XA_WIDTH), 1) // XA_HEAD_DIM

    def in_proj_fills(row, z_ref):
        x16 = x_ref[row].astype(BF16)

        def chunk_uv():
            z_ref[:, C_U:C_GA] = _dot(x16, w_in_ref[:, C_U:C_GA])

        def chunk_ssm():
            zz = _dot(x16, w_in_ref[:, C_GA:C_Q])
            z_ref[:, C_GA:C_XB] = zz[:, 0:GM_WIDTH]
            xs = zz[:, C_XB - C_GA:C_GB - C_GA]
            p0 = _aligned(row * PERM_PITCH, 8)
            for s in range(SSM_WIDTH // LANES):
                perm_ref[s, pl.ds(p0, BLOCK_T), :] = xs[:, s * LANES:(s + 1) * LANES]
            r0 = _aligned(row * BLOCK_T, BLOCK_T)
            gateb_ref[pl.ds(r0, BLOCK_T), :] = _silu(zz[:, C_GB - C_GA:])

        def chunk_attn():
            z_ref[:, C_Q:] = _dot(x16, w_in_ref[:, C_Q:])

        return chunk_uv, chunk_ssm, chunk_attn

    def branches(row, z_ref, fills):
        r0 = _aligned(row * BLOCK_T, BLOCK_T)
        q16 = (z_ref[:, C_Q:C_GX] * (XA_HEAD_DIM ** -0.5)).astype(BF16)
        kt = kt_ref[row]
        vv = v_ref[row]
        scores = [_dot(q16, jnp.where(row_head == h, kt, jnp.zeros_like(kt)))
                  for h in range(XA_HEADS)]
        fills[0]()
        fills[1]()
        o = None
        scale = None
        for h in range(XA_HEADS):
            s = scores[h]
            p = jnp.exp(s - jnp.max(s, axis=-1, keepdims=True))
            inv = 1.0 / jnp.sum(p, axis=-1, keepdims=True)
            oh = _dot(p.astype(BF16), jnp.where(col_head == h, vv, jnp.zeros_like(vv)))
            o = oh if o is None else o + oh
            scale = inv if scale is None else jnp.where(out_head == h, inv, scale)
        v = _gelu(z_ref[:, C_V:C_GA])
        mixed = []
        for h in range(GM_HEADS):
            sl = slice(h * GM_HEAD_DIM, (h + 1) * GM_HEAD_DIM)
            vh = v[:, sl]
            mu = jnp.mean(vh, axis=-1, keepdims=True)
            xc = vh - mu
            var = jnp.mean(xc * xc, axis=-1, keepdims=True)
            vn = xc * lax.rsqrt(var + LN_EPS) * gmg_ref[:, sl] + gmb_ref[:, sl]
            mixed.append(_dot(wm_ref[h], vn.astype(BF16)) + gmbias_ref[h])
        fills[2]()
        u = _gelu(z_ref[:, C_U:C_V])
        ya = u * jnp.concatenate(mixed, axis=1) * _silu(z_ref[:, C_GA:C_XB])
        y_ref[pl.ds(r0, BLOCK_T), Y_A:Y_X] = ya.astype(BF16)
        yx = o * scale * _silu(z_ref[:, C_GX:])
        y_ref[pl.ds(r0, BLOCK_T), Y_X:Y_B] = yx.astype(BF16)

    for f in in_proj_fills(0, z0_ref):
        f()

    def row_pair(kk, carry):
        k = 2 * kk
        branches(k, z0_ref, in_proj_fills(k + 1, z1_ref))
        branches(k + 1, z1_ref, in_proj_fills(k + 2, z0_ref))
        return carry

    lax.fori_loop(0, (BLOCK_B - 2) // 2, row_pair, 0)
    branches(BLOCK_B - 2, z0_ref, in_proj_fills(BLOCK_B - 1, z1_ref))

    n_slab = SSM_WIDTH // LANES

    for t in range(BLOCK_T):
        for s in range(n_slab):
            xbp_ref[t * BLOCK_B:(t + 1) * BLOCK_B, s * LANES:(s + 1) * LANES] = (
                perm_ref[s, pl.ds(t, BLOCK_B, stride=PERM_PITCH), :])

    steps = BLOCK_T // S5_CHUNKS

    def chunk_rows(c):
        return slice(c * steps * BLOCK_B, (c + 1) * steps * BLOCK_B)

    def b_proj(rows):
        def f():
            bu_ref[rows, :] = _dot(xbp_ref[rows, :].astype(BF16), wb_ref[...])
        return f

    half = ROWS // 2
    branches(BLOCK_B - 1, z1_ref,
             (b_proj(slice(0, half)), b_proj(slice(half, ROWS)), lambda: None))

    lam_r = jnp.broadcast_to(lamr_ref[...], (BLOCK_B, N_STATE))
    lam_i = jnp.broadcast_to(lami_ref[...], (BLOCK_B, N_STATE))

    state = [h_ref[:, 0:N_STATE], h_ref[:, N_STATE:]]

    def scan_chunk(c):
        hr, hi = state
        for t in range(c * steps, (c + 1) * steps):
            rows = slice(t * BLOCK_B, (t + 1) * BLOCK_B)
            br = bu_ref[rows, 0:N_STATE]
            bi = bu_ref[rows, N_STATE:]
            hr, hi = lam_r * hr - lam_i * hi + br, lam_r * hi + lam_i * hr + bi
            bu_ref[rows, 0:N_STATE] = hr
            bu_ref[rows, N_STATE:] = hi
        state[0], state[1] = hr, hi

    def c_proj(c):
        rows = chunk_rows(c)
        return _dot(bu_ref[rows, :].astype(BF16), wc_ref[...]) + d_ref[...] * xbp_ref[rows, :]

    def glu(c, ys):
        yg = _gelu(ys)
        xbp_ref[chunk_rows(c), :] = (
            yg * _sigmoid(_dot(yg.astype(BF16), gluw_ref[...]) + glub_ref[...]))

    ys_prev = None
    for c in range(S5_CHUNKS):
        scan_chunk(c)
        ys_c = c_proj(c)
        if ys_prev is not None:
            glu(c - 1, ys_prev)
        ys_prev = ys_c
    glu(S5_CHUNKS - 1, ys_prev)
    h_ref[:, 0:N_STATE] = state[0]
    h_ref[:, N_STATE:] = state[1]

    for t in range(BLOCK_T):
        for s in range(n_slab):
            perm_ref[s, pl.ds(t, BLOCK_B, stride=PERM_PITCH), :] = (
                xbp_ref[t * BLOCK_B:(t + 1) * BLOCK_B, s * LANES:(s + 1) * LANES])

    for b in range(BLOCK_B):
        ybb = jnp.concatenate(
            [perm_ref[s, b * PERM_PITCH:b * PERM_PITCH + BLOCK_T, :] for s in range(n_slab)],
            axis=1)
        rows = slice(b * BLOCK_T, (b + 1) * BLOCK_T)
        y_ref[rows, Y_B:] = (ybb * gateb_ref[rows, :]).astype(BF16)

    for b in range(BLOCK_B):
        rows = slice(b * BLOCK_T, (b + 1) * BLOCK_T)
        r = alpha * x_ref[b] + _dot(y_ref[rows, :], w_out_ref[...])
        mu = jnp.mean(r, axis=-1, keepdims=True)
        rc = r - mu
        var = jnp.mean(rc * rc, axis=-1, keepdims=True)
        o_ref[b] = rc * lax.rsqrt(var + LN_EPS) * lng_ref[...] + lnb_ref[...]


def _full(shape):
    nd = len(shape)
    return pl.BlockSpec(shape, lambda i, j, _nd=nd: (0,) * _nd)


def _of_layer(shape, layer):
    nd = len(shape)
    return pl.BlockSpec((None,) + tuple(shape),
                        lambda i, j, _l=layer, _nd=nd: (_l,) + (0,) * _nd)


def _kv_project(mem, wkt16, wv16):
    bsz, m_len, d = mem.shape
    depth = wkt16.shape[0]
    kt_sds = jax.ShapeDtypeStruct((depth, bsz, XA_WIDTH, m_len), BF16)
    v_sds = jax.ShapeDtypeStruct((depth, bsz, m_len, XA_WIDTH), BF16)
    return pl.pallas_call(
        _kv_kernel,
        grid=(depth, bsz),
        in_specs=[pl.BlockSpec((1, m_len, d), lambda l, i: (i, 0, 0)),
                  pl.BlockSpec((None, XA_WIDTH, d), lambda l, i: (l, 0, 0)),
                  pl.BlockSpec((None, d, XA_WIDTH), lambda l, i: (l, 0, 0))],
        out_specs=[pl.BlockSpec((None, 1, XA_WIDTH, m_len), lambda l, i: (l, i, 0, 0)),
                   pl.BlockSpec((None, 1, m_len, XA_WIDTH), lambda l, i: (l, i, 0, 0))],
        out_shape=[kt_sds, v_sds],
        name="kv_project",
    )(mem, wkt16, wv16)


def _layer(x, layer, kt, vv, big, small, alpha):
    bsz, seq, d = x.shape
    m_len = kt.shape[3]
    grid = (bsz // BLOCK_B, seq // BLOCK_T)
    in_specs = [
        pl.BlockSpec((BLOCK_B, BLOCK_T, d), lambda i, j: (i, j, 0)),
        pl.BlockSpec((None, BLOCK_B, XA_WIDTH, m_len), lambda i, j: (layer, i, 0, 0)),
        pl.BlockSpec((None, BLOCK_B, m_len, XA_WIDTH), lambda i, j: (layer, i, 0, 0)),
        _full((d, IN_WIDTH)), _full((d, d)),
        _full((SSM_WIDTH, 2 * N_STATE)), _full((2 * N_STATE, SSM_WIDTH)),
        _full((SSM_WIDTH, SSM_WIDTH)),
        _of_layer((GM_HEADS, CHUNK, CHUNK), layer),
        _of_layer((GM_HEADS, CHUNK, GM_HEAD_DIM), layer),
        _of_layer((1, GM_WIDTH), layer), _of_layer((1, GM_WIDTH), layer),
        _of_layer((1, N_STATE), layer), _of_layer((1, N_STATE), layer),
        _of_layer((1, SSM_WIDTH), layer), _of_layer((1, SSM_WIDTH), layer),
        _of_layer((1, d), layer), _of_layer((1, d), layer),
    ]
    scratch = [
        pltpu.VMEM((BLOCK_T, IN_WIDTH), F32),
        pltpu.VMEM((BLOCK_T, IN_WIDTH), F32),
        pltpu.VMEM((GM_HEADS, CHUNK, CHUNK), BF16),
        pltpu.VMEM((ROWS, d), BF16),
        pltpu.VMEM((SSM_WIDTH // LANES, BLOCK_B * PERM_PITCH, LANES), F32),
        pltpu.VMEM((ROWS, SSM_WIDTH), F32),
        pltpu.VMEM((ROWS, 2 * N_STATE), F32),
        pltpu.VMEM((ROWS, SSM_WIDTH), F32),
        pltpu.VMEM((BLOCK_B, 2 * N_STATE), F32),
    ]
    return pl.pallas_call(
        functools.partial(_layer_kernel, alpha),
        grid=grid,
        in_specs=in_specs,
        out_specs=pl.BlockSpec((BLOCK_B, BLOCK_T, d), lambda i, j: (i, j, 0)),
        out_shape=jax.ShapeDtypeStruct(x.shape, x.dtype),
        scratch_shapes=scratch,
        compiler_params=pltpu.CompilerParams(
            dimension_semantics=("arbitrary", "arbitrary"),
            vmem_limit_bytes=VMEM_LIMIT_BYTES),
        name="trunk_layer",
    )(x, kt, vv, big["w_in"], big["w_out"], big["wb"], big["wc"], big["glu_w"],
      small["gm_w"], small["gm_bias"], small["gm_g"], small["gm_b"], small["lam_r"],
      small["lam_i"], small["d"], small["glu_b"], small["ln_g"], small["ln_b"])


def _ssm_discretize(lam_re, lam_im, log_step, b_re, b_im, c_re, c_im):
    depth = lam_re.shape[0]
    step = jnp.exp(log_step)[:, :, None]
    mag = jnp.exp(lam_re * step)
    lbr = mag * jnp.cos(lam_im * step)
    lbi = mag * jnp.sin(lam_im * step)
    den = lam_re * lam_re + lam_im * lam_im
    fr = ((lbr - 1.0) * lam_re + lbi * lam_im) / den
    fi = (lbi * lam_re - (lbr - 1.0) * lam_im) / den
    bbr = fr[..., None] * b_re - fi[..., None] * b_im
    bbi = fr[..., None] * b_im + fi[..., None] * b_re
    eye = jnp.eye(SSM_GROUPS, dtype=F32)
    wb = jnp.concatenate(
        [jnp.einsum('lgpc,gk->lgckp', m, eye).reshape(depth, SSM_WIDTH, N_STATE)
         for m in (bbr, bbi)], axis=2)
    wc = jnp.concatenate(
        [jnp.einsum('lgcp,gk->lgpkc', m, eye).reshape(depth, N_STATE, SSM_WIDTH)
         for m in (c_re, -c_im)], axis=1)
    return lbr.reshape(depth, 1, N_STATE), lbi.reshape(depth, 1, N_STATE), wb, wc


def kernel(x, mem, w_in, gm_w_s, gm_b_s, gm_ln_g, gm_ln_b, ssm_lam_re, ssm_lam_im,
           ssm_log_step, ssm_b_re, ssm_b_im, ssm_c_re, ssm_c_im, ssm_d, glu_w, glu_b,
           xa_w_k, xa_w_v, w_out, ln_g, ln_b):
    depth = w_in.shape[0]
    alpha = float((2 * depth) ** 0.25)
    lam_r, lam_i, wb, wc = _ssm_discretize(
        ssm_lam_re, ssm_lam_im, ssm_log_step, ssm_b_re, ssm_b_im, ssm_c_re, ssm_c_im)
    small = {
        "gm_w": gm_w_s,
        "gm_bias": jnp.broadcast_to(gm_b_s[..., None], (depth, GM_HEADS, CHUNK, GM_HEAD_DIM)),
        "gm_g": gm_ln_g.reshape(depth, 1, GM_WIDTH), "gm_b": gm_ln_b.reshape(depth, 1, GM_WIDTH),
        "lam_r": lam_r, "lam_i": lam_i, "d": ssm_d.reshape(depth, 1, SSM_WIDTH),
        "glu_b": glu_b.reshape(depth, 1, SSM_WIDTH),
        "ln_g": ln_g.reshape(depth, 1, D_MODEL), "ln_b": ln_b.reshape(depth, 1, D_MODEL),
    }
    wb16, wc16, glu_w16 = wb.astype(BF16), wc.astype(BF16), glu_w.astype(BF16)
    a_end = GM_WIDTH
    b_end = GM_WIDTH + SSM_WIDTH
    kt, vv = _kv_project(mem, jnp.swapaxes(xa_w_k, 1, 2).astype(BF16), xa_w_v.astype(BF16))
    for l in range(depth):
        w_out_axb = jnp.concatenate(
            [w_out[l, :a_end], w_out[l, b_end:], w_out[l, a_end:b_end]], axis=0)
        big = {"w_in": w_in[l].astype(BF16), "w_out": w_out_axb.astype(BF16),
               "wb": wb16[l], "wc": wc16[l], "glu_w": glu_w16[l]}
        x = _layer(x, l, kt, vv, big, small, alpha)
    return x
```

```python
import functools
import math

import jax
import jax.numpy as jnp
from jax import lax
from jax.experimental import pallas as pl
from jax.experimental.pallas import tpu as pltpu

F32 = jnp.float32
BF16 = jnp.bfloat16

D_MODEL = 1024
GM_WIDTH = 512
GM_HEADS = 4
GM_HEAD_DIM = 128
CHUNK = 128
SSM_WIDTH = 256
SSM_GROUP = 16
SSM_GROUPS = 16
SSM_STATE = 64
N_STATE = SSM_GROUPS * SSM_STATE
XA_WIDTH = 256
XA_HEADS = 4
XA_HEAD_DIM = 64
IN_WIDTH = 3 * GM_WIDTH + 2 * SSM_WIDTH + 2 * XA_WIDTH
LN_EPS = 1e-5

C_U, C_V, C_GA = 0, GM_WIDTH, 2 * GM_WIDTH
C_XB = 3 * GM_WIDTH
C_GB = C_XB + SSM_WIDTH
C_Q = C_GB + SSM_WIDTH
C_GX = C_Q + XA_WIDTH
Y_A, Y_X, Y_B = 0, GM_WIDTH, GM_WIDTH + XA_WIDTH

BLOCK_B = 8
BLOCK_T = CHUNK
ROWS = BLOCK_B * BLOCK_T
S5_CHUNKS = 4
KV_BLOCK_B = 4
LANES = 128
PERM_PITCH = BLOCK_T + 8
VMEM_LIMIT_BYTES = 60 * 1024 * 1024

_LOG2E = 1.0 / math.log(2.0)
_GELU_K1 = -2.0 * math.sqrt(2.0 / math.pi) * _LOG2E
_GELU_K3 = _GELU_K1 * 0.044715


def _gelu(x):
    return x * (1.0 / (1.0 + jnp.exp2(x * (_GELU_K1 + _GELU_K3 * (x * x)))))


def _sigmoid(x):
    return 1.0 / (1.0 + jnp.exp2(x * (-_LOG2E)))


def _silu(x):
    return x * _sigmoid(x)


def _dot(a, b):
    return jnp.dot(a, b, preferred_element_type=F32)


def _aligned(v, m):
    return v if isinstance(v, int) else pl.multiple_of(v, m)


def _kv_kernel(mem_ref, wkt_ref, wv_ref, kt_ref, v_ref):
    for b in range(KV_BLOCK_B):
        m16 = mem_ref[b].astype(BF16)
        kt = lax.dot_general(wkt_ref[...], m16, (((1,), (1,)), ((), ())),
                             preferred_element_type=F32)
        kt_ref[b] = kt.astype(BF16)
        v_ref[b] = _dot(m16, wv_ref[...]).astype(BF16)


def _layer_kernel(alpha, layer,
                  x_ref, kt_ref, v_ref, w_in_ref, w_out_ref, wb_ref, wc_ref, gluw_ref,
                  gmw_ref, gmbias_ref, gmg_ref, gmb_ref, lamr_ref, lami_ref, d_ref,
                  glub_ref, lng_ref, lnb_ref,
                  o_ref,
                  z0_ref, z1_ref, wm_ref, y_ref, perm_ref, xbp_ref, bu_ref, gateb_ref, h_ref):
    j = pl.program_id(1)
    lrow = slice(layer, layer + 1)

    @pl.when(j == 0)
    def _():
        h_ref[...] = jnp.zeros_like(h_ref)

    tri = (lax.broadcasted_iota(jnp.int32, (CHUNK, CHUNK), 0)
           >= lax.broadcasted_iota(jnp.int32, (CHUNK, CHUNK), 1))
    for hp in range(GM_HEADS // 2):
        wm_ref[hp] = jnp.concatenate(
            [jnp.where(tri, gmw_ref[2 * hp + i], 0.0).astype(BF16) for i in range(2)], axis=1)
    row_head = lax.broadcasted_iota(jnp.int32, (XA_WIDTH, XA_WIDTH), 0) // XA_HEAD_DIM
    col_head = lax.broadcasted_iota(jnp.int32, (XA_WIDTH, XA_WIDTH), 1) // XA_HEAD_DIM
    out_head = lax.broadcasted_iota(jnp.int32, (BLOCK_T, XA_WIDTH), 1) // XA_HEAD_DIM

    def in_proj_fills(row, z_ref):
        x16 = x_ref[row].astype(BF16)

        def chunk_uv():
            z_ref[:, C_U:C_GA] = _dot(x16, w_in_ref[:, C_U:C_GA])

        def chunk_ssm():
            zz = _dot(x16, w_in_ref[:, C_GA:C_Q])
            z_ref[:, C_GA:C_XB] = zz[:, 0:GM_WIDTH]
            xs = zz[:, C_XB - C_GA:C_GB - C_GA]
            p0 = _aligned(row * PERM_PITCH, 8)
            for s in range(SSM_WIDTH // LANES):
                perm_ref[s, pl.ds(p0, BLOCK_T), :] = xs[:, s * LANES:(s + 1) * LANES]
            r0 = _aligned(row * BLOCK_T, BLOCK_T)
            gateb_ref[pl.ds(r0, BLOCK_T), :] = _silu(zz[:, C_GB - C_GA:])

        def chunk_attn():
            z_ref[:, C_Q:] = _dot(x16, w_in_ref[:, C_Q:])

        return chunk_uv, chunk_ssm, chunk_attn

    def branches(row, z_ref, fills):
        r0 = _aligned(row * BLOCK_T, BLOCK_T)
        q16 = (z_ref[:, C_Q:C_GX] * (XA_HEAD_DIM ** -0.5 * _LOG2E)).astype(BF16)
        kt = kt_ref[row]
        vv = v_ref[row]
        scores = [_dot(q16, jnp.where(row_head == h, kt, jnp.zeros_like(kt)))
                  for h in range(XA_HEADS)]
        fills[0]()
        fills[1]()
        o = None
        scale = None
        for h in range(XA_HEADS):
            s = scores[h]
            p = jnp.exp2(s - jnp.max(s, axis=-1, keepdims=True))
            inv = 1.0 / jnp.sum(p, axis=-1, keepdims=True)
            oh = _dot(p.astype(BF16), jnp.where(col_head == h, vv, jnp.zeros_like(vv)))
            o = oh if o is None else o + oh
            scale = inv if scale is None else jnp.where(out_head == h, inv, scale)
        v = _gelu(z_ref[:, C_V:C_GA])
        vn16 = []
        for h in range(GM_HEADS):
            sl = slice(h * GM_HEAD_DIM, (h + 1) * GM_HEAD_DIM)
            vh = v[:, sl]
            mu = jnp.mean(vh, axis=-1, keepdims=True)
            xc = vh - mu
            var = jnp.mean(xc * xc, axis=-1, keepdims=True)
            vn = xc * lax.rsqrt(var + LN_EPS) * gmg_ref[lrow, sl] + gmb_ref[lrow, sl]
            vn16.append(vn.astype(BF16))
        mixed = []
        zero = jnp.zeros((CHUNK, GM_HEAD_DIM), BF16)
        for hp in range(GM_HEADS // 2):
            v_a, v_b = vn16[2 * hp], vn16[2 * hp + 1]
            rhs = jnp.concatenate([jnp.concatenate([v_a, zero], axis=1),
                                   jnp.concatenate([zero, v_b], axis=1)], axis=0)
            mixed.append(_dot(wm_ref[hp], rhs))
        fills[2]()
        u = _gelu(z_ref[:, C_U:C_V])
        ya = u * (jnp.concatenate(mixed, axis=1) + gmbias_ref[...]) * _silu(z_ref[:, C_GA:C_XB])
        y_ref[pl.ds(r0, BLOCK_T), Y_A:Y_X] = ya.astype(BF16)
        yx = o * scale * _silu(z_ref[:, C_GX:])
        y_ref[pl.ds(r0, BLOCK_T), Y_X:Y_B] = yx.astype(BF16)

    for f in in_proj_fills(0, z0_ref):
        f()

    def row_pair(kk, carry):
        k = 2 * kk
        branches(k, z0_ref, in_proj_fills(k + 1, z1_ref))
        branches(k + 1, z1_ref, in_proj_fills(k + 2, z0_ref))
        return carry

    lax.fori_loop(0, (BLOCK_B - 2) // 2, row_pair, 0)
    branches(BLOCK_B - 2, z0_ref, in_proj_fills(BLOCK_B - 1, z1_ref))

    n_slab = SSM_WIDTH // LANES

    for t in range(BLOCK_T):
        for s in range(n_slab):
            xbp_ref[t * BLOCK_B:(t + 1) * BLOCK_B, s * LANES:(s + 1) * LANES] = (
                perm_ref[s, pl.ds(t, BLOCK_B, stride=PERM_PITCH), :])

    steps = BLOCK_T // S5_CHUNKS

    def chunk_rows(c):
        return slice(c * steps * BLOCK_B, (c + 1) * steps * BLOCK_B)

    def b_proj(rows):
        def f():
            bu_ref[rows, :] = _dot(xbp_ref[rows, :].astype(BF16), wb_ref[...])
        return f

    half = ROWS // 2
    branches(BLOCK_B - 1, z1_ref,
             (b_proj(slice(0, half)), b_proj(slice(half, ROWS)), lambda: None))

    lam_r = jnp.broadcast_to(lamr_ref[lrow, :], (BLOCK_B, N_STATE))
    lam_i = jnp.broadcast_to(lami_ref[lrow, :], (BLOCK_B, N_STATE))

    state = [h_ref[:, 0:N_STATE], h_ref[:, N_STATE:]]

    def scan_chunk(c):
        hr, hi = state
        for t in range(c * steps, (c + 1) * steps):
            rows = slice(t * BLOCK_B, (t + 1) * BLOCK_B)
            br = bu_ref[rows, 0:N_STATE]
            bi = bu_ref[rows, N_STATE:]
            hr, hi = lam_r * hr - lam_i * hi + br, lam_r * hi + lam_i * hr + bi
            bu_ref[rows, 0:N_STATE] = hr
            bu_ref[rows, N_STATE:] = hi
        state[0], state[1] = hr, hi

    def c_proj(c):
        rows = chunk_rows(c)
        return _dot(bu_ref[rows, :].astype(BF16), wc_ref[...]) + d_ref[lrow, :] * xbp_ref[rows, :]

    def glu(c, ys):
        yg = _gelu(ys)
        xbp_ref[chunk_rows(c), :] = (
            yg * _sigmoid(_dot(yg.astype(BF16), gluw_ref[...]) + glub_ref[lrow, :]))

    ys_prev = None
    for c in range(S5_CHUNKS):
        scan_chunk(c)
        ys_c = c_proj(c)
        if ys_prev is not None:
            glu(c - 1, ys_prev)
        ys_prev = ys_c
    glu(S5_CHUNKS - 1, ys_prev)
    h_ref[:, 0:N_STATE] = state[0]
    h_ref[:, N_STATE:] = state[1]

    for t in range(BLOCK_T):
        for s in range(n_slab):
            perm_ref[s, pl.ds(t, BLOCK_B, stride=PERM_PITCH), :] = (
                xbp_ref[t * BLOCK_B:(t + 1) * BLOCK_B, s * LANES:(s + 1) * LANES])

    for b in range(BLOCK_B):
        ybb = jnp.concatenate(
            [perm_ref[s, b * PERM_PITCH:b * PERM_PITCH + BLOCK_T, :] for s in range(n_slab)],
            axis=1)
        rows = slice(b * BLOCK_T, (b + 1) * BLOCK_T)
        y_ref[rows, Y_B:] = (ybb * gateb_ref[rows, :]).astype(BF16)

    for b in range(BLOCK_B):
        rows = slice(b * BLOCK_T, (b + 1) * BLOCK_T)
        r = alpha * x_ref[b] + _dot(y_ref[rows, :], w_out_ref[...])
        mu = jnp.mean(r, axis=-1, keepdims=True)
        rc = r - mu
        var = jnp.mean(rc * rc, axis=-1, keepdims=True)
        o_ref[b] = rc * lax.rsqrt(var + LN_EPS) * lng_ref[lrow, :] + lnb_ref[lrow, :]


def _full(shape):
    nd = len(shape)
    return pl.BlockSpec(shape, lambda i, j, _nd=nd: (0,) * _nd)


def _of_layer(shape, layer):
    nd = len(shape)
    return pl.BlockSpec((None,) + tuple(shape),
                        lambda i, j, _l=layer, _nd=nd: (_l,) + (0,) * _nd)


def _kv_project(mem, wkt16, wv16):
    bsz, m_len, d = mem.shape
    depth = wkt16.shape[0]
    kt_sds = jax.ShapeDtypeStruct((depth, bsz, XA_WIDTH, m_len), BF16)
    v_sds = jax.ShapeDtypeStruct((depth, bsz, m_len, XA_WIDTH), BF16)
    return pl.pallas_call(
        _kv_kernel,
        grid=(depth, bsz // KV_BLOCK_B),
        in_specs=[pl.BlockSpec((KV_BLOCK_B, m_len, d), lambda l, i: (i, 0, 0)),
                  pl.BlockSpec((None, XA_WIDTH, d), lambda l, i: (l, 0, 0)),
                  pl.BlockSpec((None, d, XA_WIDTH), lambda l, i: (l, 0, 0))],
        out_specs=[pl.BlockSpec((None, KV_BLOCK_B, XA_WIDTH, m_len), lambda l, i: (l, i, 0, 0)),
                   pl.BlockSpec((None, KV_BLOCK_B, m_len, XA_WIDTH), lambda l, i: (l, i, 0, 0))],
        out_shape=[kt_sds, v_sds],
        name="kv_project",
    )(mem, wkt16, wv16)


def _layer(x, layer, kt, vv, big, small, alpha):
    bsz, seq, d = x.shape
    depth, _, _, m_len = kt.shape
    grid = (bsz // BLOCK_B, seq // BLOCK_T)
    in_specs = [
        pl.BlockSpec((BLOCK_B, BLOCK_T, d), lambda i, j: (i, j, 0)),
        pl.BlockSpec((None, BLOCK_B, XA_WIDTH, m_len), lambda i, j: (layer, i, 0, 0)),
        pl.BlockSpec((None, BLOCK_B, m_len, XA_WIDTH), lambda i, j: (layer, i, 0, 0)),
        _full((d, IN_WIDTH)), _full((d, d)),
        _full((SSM_WIDTH, 2 * N_STATE)), _full((2 * N_STATE, SSM_WIDTH)),
        _full((SSM_WIDTH, SSM_WIDTH)),
        _of_layer((GM_HEADS, CHUNK, CHUNK), layer),
        _of_layer((CHUNK, GM_WIDTH), layer),
        _full((depth, GM_WIDTH)), _full((depth, GM_WIDTH)),
        _full((depth, N_STATE)), _full((depth, N_STATE)),
        _full((depth, SSM_WIDTH)), _full((depth, SSM_WIDTH)),
        _full((depth, d)), _full((depth, d)),
    ]
    scratch = [
        pltpu.VMEM((BLOCK_T, IN_WIDTH), F32),
        pltpu.VMEM((BLOCK_T, IN_WIDTH), F32),
        pltpu.VMEM((GM_HEADS // 2, CHUNK, 2 * CHUNK), BF16),
        pltpu.VMEM((ROWS, d), BF16),
        pltpu.VMEM((SSM_WIDTH // LANES, BLOCK_B * PERM_PITCH, LANES), F32),
        pltpu.VMEM((ROWS, SSM_WIDTH), F32),
        pltpu.VMEM((ROWS, 2 * N_STATE), F32),
        pltpu.VMEM((ROWS, SSM_WIDTH), F32),
        pltpu.VMEM((BLOCK_B, 2 * N_STATE), F32),
    ]
    return pl.pallas_call(
        functools.partial(_layer_kernel, alpha, layer),
        grid=grid,
        in_specs=in_specs,
        out_specs=pl.BlockSpec((BLOCK_B, BLOCK_T, d), lambda i, j: (i, j, 0)),
        out_shape=jax.ShapeDtypeStruct(x.shape, x.dtype),
        scratch_shapes=scratch,
        compiler_params=pltpu.CompilerParams(
            dimension_semantics=("arbitrary", "arbitrary"),
            vmem_limit_bytes=VMEM_LIMIT_BYTES),
        name="trunk_layer",
    )(x, kt, vv, big["w_in"], big["w_out"], big["wb"], big["wc"], big["glu_w"],
      small["gm_w"], small["gm_bias"], small["gm_g"], small["gm_b"], small["lam_r"],
      small["lam_i"], small["d"], small["glu_b"], small["ln_g"], small["ln_b"])


def _ssm_discretize(lam_re, lam_im, log_step, b_re, b_im, c_re, c_im):
    depth = lam_re.shape[0]
    step = jnp.exp(log_step)[:, :, None]
    mag = jnp.exp(lam_re * step)
    lbr = mag * jnp.cos(lam_im * step)
    lbi = mag * jnp.sin(lam_im * step)
    den = lam_re * lam_re + lam_im * lam_im
    fr = ((lbr - 1.0) * lam_re + lbi * lam_im) / den
    fi = (lbi * lam_re - (lbr - 1.0) * lam_im) / den
    bbr = fr[..., None] * b_re - fi[..., None] * b_im
    bbi = fr[..., None] * b_im + fi[..., None] * b_re
    eye = jnp.eye(SSM_GROUPS, dtype=F32)
    wb = jnp.concatenate(
        [jnp.einsum('lgpc,gk->lgckp', m, eye).reshape(depth, SSM_WIDTH, N_STATE)
         for m in (bbr, bbi)], axis=2)
    wc = jnp.concatenate(
        [jnp.einsum('lgcp,gk->lgpkc', m, eye).reshape(depth, N_STATE, SSM_WIDTH)
         for m in (c_re, -c_im)], axis=1)
    return lbr.reshape(depth, N_STATE), lbi.reshape(depth, N_STATE), wb, wc


def kernel(x, mem, w_in, gm_w_s, gm_b_s, gm_ln_g, gm_ln_b, ssm_lam_re, ssm_lam_im,
           ssm_log_step, ssm_b_re, ssm_b_im, ssm_c_re, ssm_c_im, ssm_d, glu_w, glu_b,
           xa_w_k, xa_w_v, w_out, ln_g, ln_b):
    depth = w_in.shape[0]
    alpha = float((2 * depth) ** 0.25)
    lam_r, lam_i, wb, wc = _ssm_discretize(
        ssm_lam_re, ssm_lam_im, ssm_log_step, ssm_b_re, ssm_b_im, ssm_c_re, ssm_c_im)
    small = {
        "gm_w": gm_w_s,
        "gm_bias": jnp.repeat(jnp.swapaxes(gm_b_s, 1, 2), GM_HEAD_DIM, axis=2),
        "gm_g": gm_ln_g.reshape(depth, GM_WIDTH), "gm_b": gm_ln_b.reshape(depth, GM_WIDTH),
        "lam_r": lam_r, "lam_i": lam_i, "d": ssm_d, "glu_b": glu_b,
        "ln_g": ln_g, "ln_b": ln_b,
    }
    wb16, wc16, glu_w16 = wb.astype(BF16), wc.astype(BF16), glu_w.astype(BF16)
    a_end = GM_WIDTH
    b_end = GM_WIDTH + SSM_WIDTH
    kt, vv = _kv_project(mem, jnp.swapaxes(xa_w_k, 1, 2).astype(BF16), xa_w_v.astype(BF16))
    for l in range(depth):
        w_out_axb = jnp.concatenate(
            [w_out[l, :a_end], w_out[l, b_end:], w_out[l, a_end:b_end]], axis=0)
        big = {"w_in": w_in[l].astype(BF16), "w_out": w_out_axb.astype(BF16),
               "wb": wb16[l], "wc": wc16[l], "glu_w": glu_w16[l]}
        x = _layer(x, l, kt, vv, big, small, alpha)
    return x
```

```python
import functools
import math

import jax
import jax.numpy as jnp
from jax import lax
from jax.experimental import pallas as pl
from jax.experimental.pallas import tpu as pltpu

F32 = jnp.float32
BF16 = jnp.bfloat16

D_MODEL = 1024
GM_WIDTH = 512
GM_HEADS = 4
GM_HEAD_DIM = 128
CHUNK = 128
SSM_WIDTH = 256
SSM_GROUP = 16
SSM_GROUPS = 16
SSM_STATE = 64
N_STATE = SSM_GROUPS * SSM_STATE
XA_WIDTH = 256
XA_HEADS = 4
XA_HEAD_DIM = 64
IN_WIDTH = 3 * GM_WIDTH + 2 * SSM_WIDTH + 2 * XA_WIDTH
LN_EPS = 1e-5

C_U, C_V, C_GA = 0, GM_WIDTH, 2 * GM_WIDTH
C_XB = 3 * GM_WIDTH
C_GB = C_XB + SSM_WIDTH
C_Q = C_GB + SSM_WIDTH
C_GX = C_Q + XA_WIDTH
Y_A, Y_B, Y_X = 0, GM_WIDTH, GM_WIDTH + SSM_WIDTH

BLOCK_B = 8
BLOCK_T = CHUNK
ROWS = BLOCK_B * BLOCK_T
ROWS_PER_ITER = 2
S5_CHUNKS = 4
KV_BLOCK_B = 4
LANES = 128
PERM_PITCH = BLOCK_T + 8
VMEM_LIMIT_BYTES = 60 * 1024 * 1024

_LOG2E = 1.0 / math.log(2.0)
_GELU_K1 = -2.0 * math.sqrt(2.0 / math.pi) * _LOG2E
_GELU_K3 = _GELU_K1 * 0.044715


def _gelu(x):
    return x * (1.0 / (1.0 + jnp.exp2(x * (_GELU_K1 + _GELU_K3 * (x * x)))))


def _sigmoid(x):
    return 1.0 / (1.0 + jnp.exp2(x * (-_LOG2E)))


def _silu(x):
    return x * _sigmoid(x)


def _dot(a, b):
    return jnp.dot(a, b, preferred_element_type=F32)


def _aligned(v, m):
    return v if isinstance(v, int) else pl.multiple_of(v, m)


def _kv_kernel(mem_ref, wkt_ref, wv_ref, kt_ref, v_ref):
    depth = wkt_ref.shape[0]
    for b in range(KV_BLOCK_B):
        m16 = mem_ref[b].astype(BF16)
        for l in range(depth):
            kt = lax.dot_general(wkt_ref[l], m16, (((1,), (1,)), ((), ())),
                                 preferred_element_type=F32)
            kt_ref[l, b] = kt.astype(BF16)
            v_ref[l, b] = _dot(m16, wv_ref[l]).astype(BF16)


def _layer_kernel(alpha, layer,
                  x_ref, kt_ref, v_ref, w_in_ref, w_out_ref, wb_ref, wc_ref, gluw_ref,
                  gmw_ref, gmbias_ref, gmg_ref, gmb_ref, lamr_ref, lami_ref, d_ref,
                  glub_ref, lng_ref, lnb_ref,
                  o_ref,
                  z0_ref, z1_ref, wm_ref, y_ref, perm_ref, xbp_ref, bu_ref, gateb_ref, h_ref):
    j = pl.program_id(1)
    lrow = slice(layer, layer + 1)
    z_refs = (z0_ref, z1_ref)
    assert len(z_refs) == ROWS_PER_ITER

    @pl.when(j == 0)
    def _():
        h_ref[...] = jnp.zeros_like(h_ref)

    tri = (lax.broadcasted_iota(jnp.int32, (CHUNK, CHUNK), 0)
           >= lax.broadcasted_iota(jnp.int32, (CHUNK, CHUNK), 1))
    for hp in range(GM_HEADS // 2):
        wm_ref[hp] = jnp.concatenate(
            [jnp.where(tri, gmw_ref[2 * hp + i], 0.0).astype(BF16) for i in range(2)], axis=1)
    row_head = lax.broadcasted_iota(jnp.int32, (XA_WIDTH, XA_WIDTH), 0) // XA_HEAD_DIM
    col_head = lax.broadcasted_iota(jnp.int32, (XA_WIDTH, XA_WIDTH), 1) // XA_HEAD_DIM
    out_head = lax.broadcasted_iota(jnp.int32, (BLOCK_T, XA_WIDTH), 1) // XA_HEAD_DIM

    def in_proj_fills(row, z_ref):
        x16 = x_ref[row].astype(BF16)

        def chunk_uv():
            z_ref[:, C_U:C_GA] = _dot(x16, w_in_ref[:, C_U:C_GA])

        def chunk_ssm():
            zz = _dot(x16, w_in_ref[:, C_GA:C_Q])
            z_ref[:, C_GA:C_XB] = zz[:, 0:GM_WIDTH]
            xs = zz[:, C_XB - C_GA:C_GB - C_GA]
            p0 = _aligned(row * PERM_PITCH, 8)
            for s in range(SSM_WIDTH // LANES):
                perm_ref[s, pl.ds(p0, BLOCK_T), :] = xs[:, s * LANES:(s + 1) * LANES]
            r0 = _aligned(row * BLOCK_T, BLOCK_T)
            gateb_ref[pl.ds(r0, BLOCK_T), :] = _silu(zz[:, C_GB - C_GA:])

        def chunk_attn():
            z_ref[:, C_Q:] = _dot(x16, w_in_ref[:, C_Q:])

        return chunk_uv, chunk_ssm, chunk_attn

    def branches(row, z_ref, fills):
        r0 = _aligned(row * BLOCK_T, BLOCK_T)
        q16 = (z_ref[:, C_Q:C_GX] * (XA_HEAD_DIM ** -0.5 * _LOG2E)).astype(BF16)
        kt = kt_ref[row]
        vv = v_ref[row]
        scores = [_dot(q16, jnp.where(row_head == h, kt, jnp.zeros_like(kt)))
                  for h in range(XA_HEADS)]
        fills[0]()
        fills[1]()
        o = None
        scale = None
        for h in range(XA_HEADS):
            s = scores[h]
            p = jnp.exp2(s - jnp.max(s, axis=-1, keepdims=True))
            inv = 1.0 / jnp.sum(p, axis=-1, keepdims=True)
            oh = _dot(p.astype(BF16), jnp.where(col_head == h, vv, jnp.zeros_like(vv)))
            o = oh if o is None else o + oh
            scale = inv if scale is None else jnp.where(out_head == h, inv, scale)
        v = _gelu(z_ref[:, C_V:C_GA])
        vn16 = []
        for h in range(GM_HEADS):
            sl = slice(h * GM_HEAD_DIM, (h + 1) * GM_HEAD_DIM)
            vh = v[:, sl]
            mu = jnp.mean(vh, axis=-1, keepdims=True)
            xc = vh - mu
            var = jnp.mean(xc * xc, axis=-1, keepdims=True)
            vn = xc * lax.rsqrt(var + LN_EPS) * gmg_ref[lrow, sl] + gmb_ref[lrow, sl]
            vn16.append(vn.astype(BF16))
        mixed = []
        zero = jnp.zeros((CHUNK, GM_HEAD_DIM), BF16)
        for hp in range(GM_HEADS // 2):
            v_a, v_b = vn16[2 * hp], vn16[2 * hp + 1]
            rhs = jnp.concatenate([jnp.concatenate([v_a, zero], axis=1),
                                   jnp.concatenate([zero, v_b], axis=1)], axis=0)
            mixed.append(_dot(wm_ref[hp], rhs))
        fills[2]()
        u = _gelu(z_ref[:, C_U:C_V])
        ya = u * (jnp.concatenate(mixed, axis=1) + gmbias_ref[...]) * _silu(z_ref[:, C_GA:C_XB])
        y_ref[pl.ds(r0, BLOCK_T), Y_A:Y_B] = ya.astype(BF16)
        yx = o * scale * _silu(z_ref[:, C_GX:])
        y_ref[pl.ds(r0, BLOCK_T), Y_X:] = yx.astype(BF16)

    n_z = len(z_refs)
    for f in in_proj_fills(0, z_refs[0]):
        f()

    def row_group(g, carry):
        k0 = ROWS_PER_ITER * g
        for i in range(ROWS_PER_ITER):
            branches(k0 + i, z_refs[i % n_z], in_proj_fills(k0 + i + 1, z_refs[(i + 1) % n_z]))
        return carry

    n_looped = (BLOCK_B - 1) // ROWS_PER_ITER * ROWS_PER_ITER
    lax.fori_loop(0, n_looped // ROWS_PER_ITER, row_group, 0)
    for k in range(n_looped, BLOCK_B - 1):
        branches(k, z_refs[k % n_z], in_proj_fills(k + 1, z_refs[(k + 1) % n_z]))

    n_slab = SSM_WIDTH // LANES

    for t in range(BLOCK_T):
        for s in range(n_slab):
            xbp_ref[t * BLOCK_B:(t + 1) * BLOCK_B, s * LANES:(s + 1) * LANES] = (
                perm_ref[s, pl.ds(t, BLOCK_B, stride=PERM_PITCH), :])

    steps = BLOCK_T // S5_CHUNKS

    def chunk_rows(c):
        return slice(c * steps * BLOCK_B, (c + 1) * steps * BLOCK_B)

    def b_proj(rows):
        def f():
            bu_ref[rows, :] = _dot(xbp_ref[rows, :].astype(BF16), wb_ref[...])
        return f

    half = ROWS // 2
    branches(BLOCK_B - 1, z_refs[(BLOCK_B - 1) % n_z],
             (b_proj(slice(0, half)), b_proj(slice(half, ROWS)), lambda: None))

    lam_r = jnp.broadcast_to(lamr_ref[lrow, :], (BLOCK_B, N_STATE))
    lam_i = jnp.broadcast_to(lami_ref[lrow, :], (BLOCK_B, N_STATE))

    state = [h_ref[:, 0:N_STATE], h_ref[:, N_STATE:]]

    def scan_chunk(c):
        hr, hi = state
        for t in range(c * steps, (c + 1) * steps):
            rows = slice(t * BLOCK_B, (t + 1) * BLOCK_B)
            br = bu_ref[rows, 0:N_STATE]
            bi = bu_ref[rows, N_STATE:]
            hr, hi = lam_r * hr - lam_i * hi + br, lam_r * hi + lam_i * hr + bi
            bu_ref[rows, 0:N_STATE] = hr
            bu_ref[rows, N_STATE:] = hi
        state[0], state[1] = hr, hi

    def c_proj(c):
        rows = chunk_rows(c)
        return _dot(bu_ref[rows, :].astype(BF16), wc_ref[...]) + d_ref[lrow, :] * xbp_ref[rows, :]

    def glu(c, ys):
        yg = _gelu(ys)
        xbp_ref[chunk_rows(c), :] = (
            yg * _sigmoid(_dot(yg.astype(BF16), gluw_ref[...]) + glub_ref[lrow, :]))

    ys_prev = None
    for c in range(S5_CHUNKS):
        scan_chunk(c)
        ys_c = c_proj(c)
        if ys_prev is not None:
            glu(c - 1, ys_prev)
        ys_prev = ys_c
    glu(S5_CHUNKS - 1, ys_prev)
    h_ref[:, 0:N_STATE] = state[0]
    h_ref[:, N_STATE:] = state[1]

    for t in range(BLOCK_T):
        for s in range(n_slab):
            perm_ref[s, pl.ds(t, BLOCK_B, stride=PERM_PITCH), :] = (
                xbp_ref[t * BLOCK_B:(t + 1) * BLOCK_B, s * LANES:(s + 1) * LANES])

    for b in range(BLOCK_B):
        ybb = jnp.concatenate(
            [perm_ref[s, b * PERM_PITCH:b * PERM_PITCH + BLOCK_T, :] for s in range(n_slab)],
            axis=1)
        rows = slice(b * BLOCK_T, (b + 1) * BLOCK_T)
        y_ref[rows, Y_B:Y_X] = (ybb * gateb_ref[rows, :]).astype(BF16)

    for b in range(BLOCK_B):
        rows = slice(b * BLOCK_T, (b + 1) * BLOCK_T)
        r = alpha * x_ref[b] + _dot(y_ref[rows, :], w_out_ref[...])
        mu = jnp.mean(r, axis=-1, keepdims=True)
        rc = r - mu
        var = jnp.mean(rc * rc, axis=-1, keepdims=True)
        o_ref[b] = rc * lax.rsqrt(var + LN_EPS) * lng_ref[lrow, :] + lnb_ref[lrow, :]


def _full(shape):
    nd = len(shape)
    return pl.BlockSpec(shape, lambda i, j, _nd=nd: (0,) * _nd)


def _of_layer(shape, layer):
    nd = len(shape)
    return pl.BlockSpec((None,) + tuple(shape),
                        lambda i, j, _l=layer, _nd=nd: (_l,) + (0,) * _nd)


def _kv_project(mem, wkt16, wv16):
    bsz, m_len, d = mem.shape
    depth = wkt16.shape[0]
    kt_sds = jax.ShapeDtypeStruct((depth, bsz, XA_WIDTH, m_len), BF16)
    v_sds = jax.ShapeDtypeStruct((depth, bsz, m_len, XA_WIDTH), BF16)
    return pl.pallas_call(
        _kv_kernel,
        grid=(bsz // KV_BLOCK_B,),
        in_specs=[pl.BlockSpec((KV_BLOCK_B, m_len, d), lambda i: (i, 0, 0)),
                  pl.BlockSpec((depth, XA_WIDTH, d), lambda i: (0, 0, 0)),
                  pl.BlockSpec((depth, d, XA_WIDTH), lambda i: (0, 0, 0))],
        out_specs=[pl.BlockSpec((depth, KV_BLOCK_B, XA_WIDTH, m_len), lambda i: (0, i, 0, 0)),
                   pl.BlockSpec((depth, KV_BLOCK_B, m_len, XA_WIDTH), lambda i: (0, i, 0, 0))],
        out_shape=[kt_sds, v_sds],
        name="kv_project",
    )(mem, wkt16, wv16)


def _layer(x, layer, kt, vv, big, small, alpha):
    bsz, seq, d = x.shape
    depth, _, _, m_len = kt.shape
    grid = (bsz // BLOCK_B, seq // BLOCK_T)
    in_specs = [
        pl.BlockSpec((BLOCK_B, BLOCK_T, d), lambda i, j: (i, j, 0)),
        pl.BlockSpec((None, BLOCK_B, XA_WIDTH, m_len), lambda i, j: (layer, i, 0, 0)),
        pl.BlockSpec((None, BLOCK_B, m_len, XA_WIDTH), lambda i, j: (layer, i, 0, 0)),
        _full((d, IN_WIDTH)), _full((d, d)),
        _full((SSM_WIDTH, 2 * N_STATE)), _full((2 * N_STATE, SSM_WIDTH)),
        _full((SSM_WIDTH, SSM_WIDTH)),
        _of_layer((GM_HEADS, CHUNK, CHUNK), layer),
        _of_layer((CHUNK, GM_WIDTH), layer),
        _full((depth, GM_WIDTH)), _full((depth, GM_WIDTH)),
        _full((depth, N_STATE)), _full((depth, N_STATE)),
        _full((depth, SSM_WIDTH)), _full((depth, SSM_WIDTH)),
        _full((depth, d)), _full((depth, d)),
    ]
    scratch = [
        pltpu.VMEM((BLOCK_T, IN_WIDTH), F32),
        pltpu.VMEM((BLOCK_T, IN_WIDTH), F32),
        pltpu.VMEM((GM_HEADS // 2, CHUNK, 2 * CHUNK), BF16),
        pltpu.VMEM((ROWS, d), BF16),
        pltpu.VMEM((SSM_WIDTH // LANES, BLOCK_B * PERM_PITCH, LANES), F32),
        pltpu.VMEM((ROWS, SSM_WIDTH), F32),
        pltpu.VMEM((ROWS, 2 * N_STATE), F32),
        pltpu.VMEM((ROWS, SSM_WIDTH), F32),
        pltpu.VMEM((BLOCK_B, 2 * N_STATE), F32),
    ]
    return pl.pallas_call(
        functools.partial(_layer_kernel, alpha, layer),
        grid=grid,
        in_specs=in_specs,
        out_specs=pl.BlockSpec((BLOCK_B, BLOCK_T, d), lambda i, j: (i, j, 0)),
        out_shape=jax.ShapeDtypeStruct(x.shape, x.dtype),
        scratch_shapes=scratch,
        compiler_params=pltpu.CompilerParams(
            dimension_semantics=("arbitrary", "arbitrary"),
            vmem_limit_bytes=VMEM_LIMIT_BYTES),
        name="trunk_layer",
    )(x, kt, vv, big["w_in"], big["w_out"], big["wb"], big["wc"], big["glu_w"],
      small["gm_w"], small["gm_bias"], small["gm_g"], small["gm_b"], small["lam_r"],
      small["lam_i"], small["d"], small["glu_b"], small["ln_g"], small["ln_b"])


def _ssm_discretize(lam_re, lam_im, log_step, b_re, b_im, c_re, c_im):
    depth = lam_re.shape[0]
    step = jnp.exp(log_step)[:, :, None]
    mag = jnp.exp(lam_re * step)
    lbr = mag * jnp.cos(lam_im * step)
    lbi = mag * jnp.sin(lam_im * step)
    den = lam_re * lam_re + lam_im * lam_im
    fr = ((lbr - 1.0) * lam_re + lbi * lam_im) / den
    fi = (lbi * lam_re - (lbr - 1.0) * lam_im) / den
    bbr = fr[..., None] * b_re - fi[..., None] * b_im
    bbi = fr[..., None] * b_im + fi[..., None] * b_re
    eye = jnp.eye(SSM_GROUPS, dtype=F32)
    wb = jnp.concatenate(
        [jnp.einsum('lgpc,gk->lgckp', m, eye).reshape(depth, SSM_WIDTH, N_STATE)
         for m in (bbr, bbi)], axis=2)
    wc = jnp.concatenate(
        [jnp.einsum('lgcp,gk->lgpkc', m, eye).reshape(depth, N_STATE, SSM_WIDTH)
         for m in (c_re, -c_im)], axis=1)
    return lbr.reshape(depth, N_STATE), lbi.reshape(depth, N_STATE), wb, wc


def kernel(x, mem, w_in, gm_w_s, gm_b_s, gm_ln_g, gm_ln_b, ssm_lam_re, ssm_lam_im,
           ssm_log_step, ssm_b_re, ssm_b_im, ssm_c_re, ssm_c_im, ssm_d, glu_w, glu_b,
           xa_w_k, xa_w_v, w_out, ln_g, ln_b):
    depth = w_in.shape[0]
    alpha = float((2 * depth) ** 0.25)
    lam_r, lam_i, wb, wc = _ssm_discretize(
        ssm_lam_re, ssm_lam_im, ssm_log_step, ssm_b_re, ssm_b_im, ssm_c_re, ssm_c_im)
    small = {
        "gm_w": gm_w_s,
        "gm_bias": jnp.repeat(jnp.swapaxes(gm_b_s, 1, 2), GM_HEAD_DIM, axis=2),
        "gm_g": gm_ln_g.reshape(depth, GM_WIDTH), "gm_b": gm_ln_b.reshape(depth, GM_WIDTH),
        "lam_r": lam_r, "lam_i": lam_i, "d": ssm_d, "glu_b": glu_b,
        "ln_g": ln_g, "ln_b": ln_b,
    }
    w_in16, w_out16 = w_in.astype(BF16), w_out.astype(BF16)
    wb16, wc16, glu_w16 = wb.astype(BF16), wc.astype(BF16), glu_w.astype(BF16)
    kt, vv = _kv_project(mem, jnp.swapaxes(xa_w_k, 1, 2).astype(BF16), xa_w_v.astype(BF16))
    for l in range(depth):
        big = {"w_in": w_in16[l], "w_out": w_out16[l],
               "wb": wb16[l], "wc": wc16[l], "glu_w": glu_w16[l]}
        x = _layer(x, l, kt, vv, big, small, alpha)
    return x
```

```python
import functools
import math

import jax
import jax.numpy as jnp
from jax import lax
from jax.experimental import pallas as pl
from jax.experimental.pallas import tpu as pltpu

F32 = jnp.float32
BF16 = jnp.bfloat16

D_MODEL = 1024
GM_WIDTH = 512
GM_HEADS = 4
GM_HEAD_DIM = 128
CHUNK = 128
SSM_WIDTH = 256
SSM_GROUP = 16
SSM_GROUPS = 16
SSM_STATE = 64
N_STATE = SSM_GROUPS * SSM_STATE
XA_WIDTH = 256
XA_HEADS = 4
XA_HEAD_DIM = 64
IN_WIDTH = 3 * GM_WIDTH + 2 * SSM_WIDTH + 2 * XA_WIDTH
LN_EPS = 1e-5

C_U, C_V, C_GA = 0, GM_WIDTH, 2 * GM_WIDTH
C_XB = 3 * GM_WIDTH
C_GB = C_XB + SSM_WIDTH
C_Q = C_GB + SSM_WIDTH
C_GX = C_Q + XA_WIDTH
Y_A, Y_B, Y_X = 0, GM_WIDTH, GM_WIDTH + SSM_WIDTH

BLOCK_B = 8
BLOCK_T = CHUNK
ROWS = BLOCK_B * BLOCK_T
ROWS_PER_ITER = 2
S5_CHUNKS = 4
KV_BLOCK_B = 4
LANES = 128
PERM_PITCH = BLOCK_T + 8
VMEM_LIMIT_BYTES = 60 * 1024 * 1024

_LOG2E = 1.0 / math.log(2.0)
_GELU_K1 = -2.0 * math.sqrt(2.0 / math.pi) * _LOG2E
_GELU_K3 = _GELU_K1 * 0.044715


def _gelu(x):
    return x * (1.0 / (1.0 + jnp.exp2(x * (_GELU_K1 + _GELU_K3 * (x * x)))))


def _sigmoid(x):
    return 1.0 / (1.0 + jnp.exp2(x * (-_LOG2E)))


def _silu(x):
    return x * _sigmoid(x)


def _dot(a, b):
    return jnp.dot(a, b, preferred_element_type=F32)


def _aligned(v, m):
    return v if isinstance(v, int) else pl.multiple_of(v, m)


def _kv_kernel(mem_ref, wkt_ref, wv_ref, kt_ref, v_ref):
    depth = wkt_ref.shape[0]
    for b in range(KV_BLOCK_B):
        m16 = mem_ref[b].astype(BF16)
        for l in range(depth):
            kt = lax.dot_general(wkt_ref[l], m16, (((1,), (1,)), ((), ())),
                                 preferred_element_type=F32)
            kt_ref[l, b] = kt.astype(BF16)
            v_ref[l, b] = _dot(m16, wv_ref[l]).astype(BF16)


def _layer_kernel(alpha, layer,
                  x_ref, xn_ref, kt_ref, v_ref, w_in_ref, w_out_ref, wb_ref, wc_ref, gluw_ref,
                  gmw_ref, gmbias_ref, gmg_ref, gmb_ref, lamr_ref, lami_ref, d_ref,
                  glub_ref, lng_ref, lnb_ref,
                  o_ref,
                  z0_ref, z1_ref, wm_ref, y_ref, perm_ref, xbp_ref, bu_ref, gateb_ref, h_ref):
    j = pl.program_id(1)
    lrow = slice(layer, layer + 1)
    z_refs = (z0_ref, z1_ref)
    assert len(z_refs) == ROWS_PER_ITER

    @pl.when(j == 0)
    def _():
        h_ref[...] = jnp.zeros_like(h_ref)

    tri = (lax.broadcasted_iota(jnp.int32, (CHUNK, CHUNK), 0)
           >= lax.broadcasted_iota(jnp.int32, (CHUNK, CHUNK), 1))
    for hp in range(GM_HEADS // 2):
        wm_ref[hp] = jnp.concatenate(
            [jnp.where(tri, gmw_ref[2 * hp + i], 0.0).astype(BF16) for i in range(2)], axis=1)
    row_head = lax.broadcasted_iota(jnp.int32, (XA_WIDTH, XA_WIDTH), 0) // XA_HEAD_DIM
    col_head = lax.broadcasted_iota(jnp.int32, (XA_WIDTH, XA_WIDTH), 1) // XA_HEAD_DIM
    out_head = lax.broadcasted_iota(jnp.int32, (BLOCK_T, XA_WIDTH), 1) // XA_HEAD_DIM

    def in_proj_fills(row, z_ref, x_row=None):
        x16 = (x_ref[row] if x_row is None else x_row).astype(BF16)

        def chunk_uv():
            z_ref[:, C_U:C_GA] = _dot(x16, w_in_ref[:, C_U:C_GA])

        def chunk_ssm():
            zz = _dot(x16, w_in_ref[:, C_GA:C_Q])
            z_ref[:, C_GA:C_XB] = zz[:, 0:GM_WIDTH]
            xs = zz[:, C_XB - C_GA:C_GB - C_GA]
            p0 = _aligned(row * PERM_PITCH, 8)
            for s in range(SSM_WIDTH // LANES):
                perm_ref[s, pl.ds(p0, BLOCK_T), :] = xs[:, s * LANES:(s + 1) * LANES]
            r0 = _aligned(row * BLOCK_T, BLOCK_T)
            gateb_ref[pl.ds(r0, BLOCK_T), :] = _silu(zz[:, C_GB - C_GA:])

        def chunk_attn():
            z_ref[:, C_Q:] = _dot(x16, w_in_ref[:, C_Q:])

        return chunk_uv, chunk_ssm, chunk_attn

    def branches(row, z_ref, fills):
        r0 = _aligned(row * BLOCK_T, BLOCK_T)
        q16 = (z_ref[:, C_Q:C_GX] * (XA_HEAD_DIM ** -0.5 * _LOG2E)).astype(BF16)
        kt = kt_ref[row]
        vv = v_ref[row]
        scores = [_dot(q16, jnp.where(row_head == h, kt, jnp.zeros_like(kt)))
                  for h in range(XA_HEADS)]
        fills[0]()
        fills[1]()
        o = None
        scale = None
        for h in range(XA_HEADS):
            s = scores[h]
            p = jnp.exp2(s - jnp.max(s, axis=-1, keepdims=True))
            inv = 1.0 / jnp.sum(p, axis=-1, keepdims=True)
            oh = _dot(p.astype(BF16), jnp.where(col_head == h, vv, jnp.zeros_like(vv)))
            o = oh if o is None else o + oh
            scale = inv if scale is None else jnp.where(out_head == h, inv, scale)
        v = _gelu(z_ref[:, C_V:C_GA])
        vn16 = []
        for h in range(GM_HEADS):
            sl = slice(h * GM_HEAD_DIM, (h + 1) * GM_HEAD_DIM)
            vh = v[:, sl]
            mu = jnp.mean(vh, axis=-1, keepdims=True)
            xc = vh - mu
            var = jnp.mean(xc * xc, axis=-1, keepdims=True)
            vn = xc * lax.rsqrt(var + LN_EPS) * gmg_ref[lrow, sl] + gmb_ref[lrow, sl]
            vn16.append(vn.astype(BF16))
        mixed = []
        zero = jnp.zeros((CHUNK, GM_HEAD_DIM), BF16)
        for hp in range(GM_HEADS // 2):
            v_a, v_b = vn16[2 * hp], vn16[2 * hp + 1]
            rhs = jnp.concatenate([jnp.concatenate([v_a, zero], axis=1),
                                   jnp.concatenate([zero, v_b], axis=1)], axis=0)
            mixed.append(_dot(wm_ref[hp], rhs))
        fills[2]()
        u = _gelu(z_ref[:, C_U:C_V])
        ya = u * (jnp.concatenate(mixed, axis=1) + gmbias_ref[...]) * _silu(z_ref[:, C_GA:C_XB])
        y_ref[pl.ds(r0, BLOCK_T), Y_A:Y_B] = ya.astype(BF16)
        yx = o * scale * _silu(z_ref[:, C_GX:])
        y_ref[pl.ds(r0, BLOCK_T), Y_X:] = yx.astype(BF16)

    n_z = len(z_refs)

    @pl.when((pl.program_id(0) == 0) & (j == 0))
    def _():
        for f in in_proj_fills(0, z_refs[0]):
            f()

    def row_group(g, carry):
        k0 = ROWS_PER_ITER * g
        for i in range(ROWS_PER_ITER):
            branches(k0 + i, z_refs[i % n_z], in_proj_fills(k0 + i + 1, z_refs[(i + 1) % n_z]))
        return carry

    n_looped = (BLOCK_B - 1) // ROWS_PER_ITER * ROWS_PER_ITER
    lax.fori_loop(0, n_looped // ROWS_PER_ITER, row_group, 0)
    for k in range(n_looped, BLOCK_B - 1):
        branches(k, z_refs[k % n_z], in_proj_fills(k + 1, z_refs[(k + 1) % n_z]))

    n_slab = SSM_WIDTH // LANES

    for t in range(BLOCK_T):
        for s in range(n_slab):
            xbp_ref[t * BLOCK_B:(t + 1) * BLOCK_B, s * LANES:(s + 1) * LANES] = (
                perm_ref[s, pl.ds(t, BLOCK_B, stride=PERM_PITCH), :])

    steps = BLOCK_T // S5_CHUNKS

    def chunk_rows(c):
        return slice(c * steps * BLOCK_B, (c + 1) * steps * BLOCK_B)

    def b_proj(rows):
        def f():
            bu_ref[rows, :] = _dot(xbp_ref[rows, :].astype(BF16), wb_ref[...])
        return f

    half = ROWS // 2
    branches(BLOCK_B - 1, z_refs[(BLOCK_B - 1) % n_z],
             (b_proj(slice(0, half)), b_proj(slice(half, ROWS)), lambda: None))

    lam_r = jnp.broadcast_to(lamr_ref[lrow, :], (BLOCK_B, N_STATE))
    lam_i = jnp.broadcast_to(lami_ref[lrow, :], (BLOCK_B, N_STATE))

    state = [h_ref[:, 0:N_STATE], h_ref[:, N_STATE:]]

    def scan_chunk(c):
        hr, hi = state
        for t in range(c * steps, (c + 1) * steps):
            rows = slice(t * BLOCK_B, (t + 1) * BLOCK_B)
            br = bu_ref[rows, 0:N_STATE]
            bi = bu_ref[rows, N_STATE:]
            hr, hi = lam_r * hr - lam_i * hi + br, lam_r * hi + lam_i * hr + bi
            bu_ref[rows, 0:N_STATE] = hr
            bu_ref[rows, N_STATE:] = hi
        state[0], state[1] = hr, hi

    def c_proj(c):
        rows = chunk_rows(c)
        return _dot(bu_ref[rows, :].astype(BF16), wc_ref[...]) + d_ref[lrow, :] * xbp_ref[rows, :]

    def glu(c, ys):
        yg = _gelu(ys)
        xbp_ref[chunk_rows(c), :] = (
            yg * _sigmoid(_dot(yg.astype(BF16), gluw_ref[...]) + glub_ref[lrow, :]))

    ys_prev = None
    for c in range(S5_CHUNKS):
        scan_chunk(c)
        ys_c = c_proj(c)
        if ys_prev is not None:
            glu(c - 1, ys_prev)
        ys_prev = ys_c
    glu(S5_CHUNKS - 1, ys_prev)
    h_ref[:, 0:N_STATE] = state[0]
    h_ref[:, N_STATE:] = state[1]

    for t in range(BLOCK_T):
        for s in range(n_slab):
            perm_ref[s, pl.ds(t, BLOCK_B, stride=PERM_PITCH), :] = (
                xbp_ref[t * BLOCK_B:(t + 1) * BLOCK_B, s * LANES:(s + 1) * LANES])

    for b in range(BLOCK_B):
        ybb = jnp.concatenate(
            [perm_ref[s, b * PERM_PITCH:b * PERM_PITCH + BLOCK_T, :] for s in range(n_slab)],
            axis=1)
        rows = slice(b * BLOCK_T, (b + 1) * BLOCK_T)
        y_ref[rows, Y_B:Y_X] = (ybb * gateb_ref[rows, :]).astype(BF16)

    for b in range(BLOCK_B):
        rows = slice(b * BLOCK_T, (b + 1) * BLOCK_T)
        r = alpha * x_ref[b] + _dot(y_ref[rows, :], w_out_ref[...])
        if b == BLOCK_B - 1:
            for f in in_proj_fills(0, z_refs[0], x_row=xn_ref[0]):
                f()
        mu = jnp.mean(r, axis=-1, keepdims=True)
        rc = r - mu
        var = jnp.mean(rc * rc, axis=-1, keepdims=True)
        o_ref[b] = rc * lax.rsqrt(var + LN_EPS) * lng_ref[lrow, :] + lnb_ref[lrow, :]


def _full(shape):
    nd = len(shape)
    return pl.BlockSpec(shape, lambda i, j, _nd=nd: (0,) * _nd)


def _of_layer(shape, layer):
    nd = len(shape)
    return pl.BlockSpec((None,) + tuple(shape),
                        lambda i, j, _l=layer, _nd=nd: (_l,) + (0,) * _nd)


def _kv_project(mem, wkt16, wv16):
    bsz, m_len, d = mem.shape
    depth = wkt16.shape[0]
    kt_sds = jax.ShapeDtypeStruct((depth, bsz, XA_WIDTH, m_len), BF16)
    v_sds = jax.ShapeDtypeStruct((depth, bsz, m_len, XA_WIDTH), BF16)
    return pl.pallas_call(
        _kv_kernel,
        grid=(bsz // KV_BLOCK_B,),
        in_specs=[pl.BlockSpec((KV_BLOCK_B, m_len, d), lambda i: (i, 0, 0)),
                  pl.BlockSpec((depth, XA_WIDTH, d), lambda i: (0, 0, 0)),
                  pl.BlockSpec((depth, d, XA_WIDTH), lambda i: (0, 0, 0))],
        out_specs=[pl.BlockSpec((depth, KV_BLOCK_B, XA_WIDTH, m_len), lambda i: (0, i, 0, 0)),
                   pl.BlockSpec((depth, KV_BLOCK_B, m_len, XA_WIDTH), lambda i: (0, i, 0, 0))],
        out_shape=[kt_sds, v_sds],
        name="kv_project",
    )(mem, wkt16, wv16)


def _layer(x, layer, kt, vv, big, small, alpha):
    bsz, seq, d = x.shape
    depth, _, _, m_len = kt.shape
    grid = (bsz // BLOCK_B, seq // BLOCK_T)
    n_steps = grid[0] * grid[1]

    def next_row0(i, j):
        s = jnp.minimum(i * grid[1] + j + 1, n_steps - 1)
        return (s // grid[1]) * BLOCK_B, s % grid[1], 0

    in_specs = [
        pl.BlockSpec((BLOCK_B, BLOCK_T, d), lambda i, j: (i, j, 0)),
        pl.BlockSpec((1, BLOCK_T, d), next_row0),
        pl.BlockSpec((None, BLOCK_B, XA_WIDTH, m_len), lambda i, j: (layer, i, 0, 0)),
        pl.BlockSpec((None, BLOCK_B, m_len, XA_WIDTH), lambda i, j: (layer, i, 0, 0)),
        _full((d, IN_WIDTH)), _full((d, d)),
        _full((SSM_WIDTH, 2 * N_STATE)), _full((2 * N_STATE, SSM_WIDTH)),
        _full((SSM_WIDTH, SSM_WIDTH)),
        _of_layer((GM_HEADS, CHUNK, CHUNK), layer),
        _of_layer((CHUNK, GM_WIDTH), layer),
        _full((depth, GM_WIDTH)), _full((depth, GM_WIDTH)),
        _full((depth, N_STATE)), _full((depth, N_STATE)),
        _full((depth, SSM_WIDTH)), _full((depth, SSM_WIDTH)),
        _full((depth, d)), _full((depth, d)),
    ]
    scratch = [
        pltpu.VMEM((BLOCK_T, IN_WIDTH), F32),
        pltpu.VMEM((BLOCK_T, IN_WIDTH), F32),
        pltpu.VMEM((GM_HEADS // 2, CHUNK, 2 * CHUNK), BF16),
        pltpu.VMEM((ROWS, d), BF16),
        pltpu.VMEM((SSM_WIDTH // LANES, BLOCK_B * PERM_PITCH, LANES), F32),
        pltpu.VMEM((ROWS, SSM_WIDTH), F32),
        pltpu.VMEM((ROWS, 2 * N_STATE), F32),
        pltpu.VMEM((ROWS, SSM_WIDTH), F32),
        pltpu.VMEM((BLOCK_B, 2 * N_STATE), F32),
    ]
    return pl.pallas_call(
        functools.partial(_layer_kernel, alpha, layer),
        grid=grid,
        in_specs=in_specs,
        out_specs=pl.BlockSpec((BLOCK_B, BLOCK_T, d), lambda i, j: (i, j, 0)),
        out_shape=jax.ShapeDtypeStruct(x.shape, x.dtype),
        scratch_shapes=scratch,
        compiler_params=pltpu.CompilerParams(
            dimension_semantics=("arbitrary", "arbitrary"),
            vmem_limit_bytes=VMEM_LIMIT_BYTES),
        name="trunk_layer",
    )(x, x, kt, vv, big["w_in"], big["w_out"], big["wb"], big["wc"], big["glu_w"],
      small["gm_w"], small["gm_bias"], small["gm_g"], small["gm_b"], small["lam_r"],
      small["lam_i"], small["d"], small["glu_b"], small["ln_g"], small["ln_b"])


def _ssm_discretize(lam_re, lam_im, log_step, b_re, b_im, c_re, c_im):
    depth = lam_re.shape[0]
    step = jnp.exp(log_step)[:, :, None]
    mag = jnp.exp(lam_re * step)
    lbr = mag * jnp.cos(lam_im * step)
    lbi = mag * jnp.sin(lam_im * step)
    den = lam_re * lam_re + lam_im * lam_im
    fr = ((lbr - 1.0) * lam_re + lbi * lam_im) / den
    fi = (lbi * lam_re - (lbr - 1.0) * lam_im) / den
    bbr = fr[..., None] * b_re - fi[..., None] * b_im
    bbi = fr[..., None] * b_im + fi[..., None] * b_re
    eye = jnp.eye(SSM_GROUPS, dtype=F32)
    wb = jnp.concatenate(
        [jnp.einsum('lgpc,gk->lgckp', m, eye).reshape(depth, SSM_WIDTH, N_STATE)
         for m in (bbr, bbi)], axis=2)
    wc = jnp.concatenate(
        [jnp.einsum('lgcp,gk->lgpkc', m, eye).reshape(depth, N_STATE, SSM_WIDTH)
         for m in (c_re, -c_im)], axis=1)
    return lbr.reshape(depth, N_STATE), lbi.reshape(depth, N_STATE), wb, wc


def kernel(x, mem, w_in, gm_w_s, gm_b_s, gm_ln_g, gm_ln_b, ssm_lam_re, ssm_lam_im,
           ssm_log_step, ssm_b_re, ssm_b_im, ssm_c_re, ssm_c_im, ssm_d, glu_w, glu_b,
           xa_w_k, xa_w_v, w_out, ln_g, ln_b):
    depth = w_in.shape[0]
    alpha = float((2 * depth) ** 0.25)
    lam_r, lam_i, wb, wc = _ssm_discretize(
        ssm_lam_re, ssm_lam_im, ssm_log_step, ssm_b_re, ssm_b_im, ssm_c_re, ssm_c_im)
    small = {
        "gm_w": gm_w_s,
        "gm_bias": jnp.repeat(jnp.swapaxes(gm_b_s, 1, 2), GM_HEAD_DIM, axis=2),
        "gm_g": gm_ln_g.reshape(depth, GM_WIDTH), "gm_b": gm_ln_b.reshape(depth, GM_WIDTH),
        "lam_r": lam_r, "lam_i": lam_i, "d": ssm_d, "glu_b": glu_b,
        "ln_g": ln_g, "ln_b": ln_b,
    }
    w_in16, w_out16 = w_in.astype(BF16), w_out.astype(BF16)
    wb16, wc16, glu_w16 = wb.astype(BF16), wc.astype(BF16), glu_w.astype(BF16)
    kt, vv = _kv_project(mem, jnp.swapaxes(xa_w_k, 1, 2).astype(BF16), xa_w_v.astype(BF16))
    for l in range(depth):
        big = {"w_in": w_in16[l], "w_out": w_out16[l],
               "wb": wb16[l], "wc": wc16[l], "glu_w": glu_w16[l]}
        x = _layer(x, l, kt, vv, big, small, alpha)
    return x
```

```python
import functools
import math

import jax
import jax.numpy as jnp
from jax import lax
from jax.experimental import pallas as pl
from jax.experimental.pallas import tpu as pltpu

F32 = jnp.float32
BF16 = jnp.bfloat16

D_MODEL = 1024
GM_WIDTH = 512
GM_HEADS = 4
GM_HEAD_DIM = 128
CHUNK = 128
SSM_WIDTH = 256
SSM_GROUP = 16
SSM_GROUPS = 16
SSM_STATE = 64
N_STATE = SSM_GROUPS * SSM_STATE
XA_WIDTH = 256
XA_HEADS = 4
XA_HEAD_DIM = 64
IN_WIDTH = 3 * GM_WIDTH + 2 * SSM_WIDTH + 2 * XA_WIDTH
LN_EPS = 1e-5

C_U, C_V, C_GA = 0, GM_WIDTH, 2 * GM_WIDTH
C_XB = 3 * GM_WIDTH
C_GB = C_XB + SSM_WIDTH
C_Q = C_GB + SSM_WIDTH
C_GX = C_Q + XA_WIDTH
Y_A, Y_B, Y_X = 0, GM_WIDTH, GM_WIDTH + SSM_WIDTH

BLOCK_B = 8
BLOCK_T = CHUNK
ROWS = BLOCK_B * BLOCK_T
ROWS_PER_ITER = 2
S5_CHUNKS = 4
S5_STRIDE = 2
KD_OFFSETS = tuple(SSM_WIDTH * j * (j + 1) // 2 for j in range(S5_STRIDE - 1))
KD_ROWS = SSM_WIDTH * S5_STRIDE * (S5_STRIDE - 1) // 2
KV_BLOCK_B = 4
LANES = 128
PERM_PITCH = BLOCK_T + 8
VMEM_LIMIT_BYTES = 60 * 1024 * 1024

_LOG2E = 1.0 / math.log(2.0)
_GELU_K1 = -2.0 * math.sqrt(2.0 / math.pi) * _LOG2E
_GELU_K3 = _GELU_K1 * 0.044715


def _gelu(x):
    return x * (1.0 / (1.0 + jnp.exp2(x * (_GELU_K1 + _GELU_K3 * (x * x)))))


def _sigmoid(x):
    return 1.0 / (1.0 + jnp.exp2(x * (-_LOG2E)))


def _silu(x):
    return x * _sigmoid(x)


def _dot(a, b):
    return jnp.dot(a, b, preferred_element_type=F32)


def _aligned(v, m):
    return v if isinstance(v, int) else pl.multiple_of(v, m)


def _kv_kernel(mem_ref, wkt_ref, wv_ref, kt_ref, v_ref):
    depth = wkt_ref.shape[0]
    for b in range(KV_BLOCK_B):
        m16 = mem_ref[b].astype(BF16)
        for l in range(depth):
            kt = lax.dot_general(wkt_ref[l], m16, (((1,), (1,)), ((), ())),
                                 preferred_element_type=F32)
            kt_ref[l, b] = kt.astype(BF16)
            v_ref[l, b] = _dot(m16, wv_ref[l]).astype(BF16)


def _layer_kernel(alpha, layer,
                  x_ref, xn_ref, kt_ref, v_ref, w_in_ref, w_out_ref, wbs_ref, wc_ref, wcp_ref,
                  kd_ref, gluw_ref,
                  gmw_ref, gmbias_ref, gmg_ref, gmb_ref, lamr_ref, lami_ref, d_ref,
                  glub_ref, lng_ref, lnb_ref,
                  o_ref,
                  z0_ref, z1_ref, wm_ref, y_ref, perm_ref, xbp_ref, hb_ref, gateb_ref):
    j = pl.program_id(1)
    lrow = slice(layer, layer + 1)
    z_refs = (z0_ref, z1_ref)
    assert len(z_refs) == ROWS_PER_ITER

    @pl.when(j == 0)
    def _():
        hb_ref[0:BLOCK_B, :] = jnp.zeros((BLOCK_B, 2 * N_STATE), F32)

    tri = (lax.broadcasted_iota(jnp.int32, (CHUNK, CHUNK), 0)
           >= lax.broadcasted_iota(jnp.int32, (CHUNK, CHUNK), 1))
    for hp in range(GM_HEADS // 2):
        wm_ref[hp] = jnp.concatenate(
            [jnp.where(tri, gmw_ref[2 * hp + i], 0.0).astype(BF16) for i in range(2)], axis=1)
    row_head = lax.broadcasted_iota(jnp.int32, (XA_WIDTH, XA_WIDTH), 0) // XA_HEAD_DIM
    col_head = lax.broadcasted_iota(jnp.int32, (XA_WIDTH, XA_WIDTH), 1) // XA_HEAD_DIM
    out_head = lax.broadcasted_iota(jnp.int32, (BLOCK_T, XA_WIDTH), 1) // XA_HEAD_DIM

    def in_proj_fills(row, z_ref, x_row=None):
        x16 = (x_ref[row] if x_row is None else x_row).astype(BF16)

        def chunk_uv():
            z_ref[:, C_U:C_GA] = _dot(x16, w_in_ref[:, C_U:C_GA])

        def chunk_ssm():
            zz = _dot(x16, w_in_ref[:, C_GA:C_Q])
            z_ref[:, C_GA:C_XB] = zz[:, 0:GM_WIDTH]
            xs = zz[:, C_XB - C_GA:C_GB - C_GA]
            p0 = _aligned(row * PERM_PITCH, 8)
            for s in range(SSM_WIDTH // LANES):
                perm_ref[s, pl.ds(p0, BLOCK_T), :] = xs[:, s * LANES:(s + 1) * LANES]
            r0 = _aligned(row * BLOCK_T, BLOCK_T)
            gateb_ref[pl.ds(r0, BLOCK_T), :] = _silu(zz[:, C_GB - C_GA:])

        def chunk_attn():
            z_ref[:, C_Q:] = _dot(x16, w_in_ref[:, C_Q:])

        return chunk_uv, chunk_ssm, chunk_attn

    def branches(row, z_ref, fills):
        r0 = _aligned(row * BLOCK_T, BLOCK_T)
        q16 = (z_ref[:, C_Q:C_GX] * (XA_HEAD_DIM ** -0.5 * _LOG2E)).astype(BF16)
        kt = kt_ref[row]
        vv = v_ref[row]
        scores = [_dot(q16, jnp.where(row_head == h, kt, jnp.zeros_like(kt)))
                  for h in range(XA_HEADS)]
        fills[0]()
        fills[1]()
        o = None
        scale = None
        for h in range(XA_HEADS):
            s = scores[h]
            p = jnp.exp2(s - jnp.max(s, axis=-1, keepdims=True))
            inv = 1.0 / jnp.sum(p, axis=-1, keepdims=True)
            oh = _dot(p.astype(BF16), jnp.where(col_head == h, vv, jnp.zeros_like(vv)))
            o = oh if o is None else o + oh
            scale = inv if scale is None else jnp.where(out_head == h, inv, scale)
        v = _gelu(z_ref[:, C_V:C_GA])
        vn16 = []
        for h in range(GM_HEADS):
            sl = slice(h * GM_HEAD_DIM, (h + 1) * GM_HEAD_DIM)
            vh = v[:, sl]
            mu = jnp.mean(vh, axis=-1, keepdims=True)
            xc = vh - mu
            var = jnp.mean(xc * xc, axis=-1, keepdims=True)
            vn = xc * lax.rsqrt(var + LN_EPS) * gmg_ref[lrow, sl] + gmb_ref[lrow, sl]
            vn16.append(vn.astype(BF16))
        mixed = []
        zero = jnp.zeros((CHUNK, GM_HEAD_DIM), BF16)
        for hp in range(GM_HEADS // 2):
            v_a, v_b = vn16[2 * hp], vn16[2 * hp + 1]
            rhs = jnp.concatenate([jnp.concatenate([v_a, zero], axis=1),
                                   jnp.concatenate([zero, v_b], axis=1)], axis=0)
            mixed.append(_dot(wm_ref[hp], rhs))
        fills[2]()
        u = _gelu(z_ref[:, C_U:C_V])
        ya = u * (jnp.concatenate(mixed, axis=1) + gmbias_ref[...]) * _silu(z_ref[:, C_GA:C_XB])
        y_ref[pl.ds(r0, BLOCK_T), Y_A:Y_B] = ya.astype(BF16)
        yx = o * scale * _silu(z_ref[:, C_GX:])
        y_ref[pl.ds(r0, BLOCK_T), Y_X:] = yx.astype(BF16)

    n_z = len(z_refs)

    @pl.when((pl.program_id(0) == 0) & (j == 0))
    def _():
        for f in in_proj_fills(0, z_refs[0]):
            f()

    def row_group(g, carry):
        k0 = ROWS_PER_ITER * g
        for i in range(ROWS_PER_ITER):
            branches(k0 + i, z_refs[i % n_z], in_proj_fills(k0 + i + 1, z_refs[(i + 1) % n_z]))
        return carry

    n_looped = (BLOCK_B - 1) // ROWS_PER_ITER * ROWS_PER_ITER
    lax.fori_loop(0, n_looped // ROWS_PER_ITER, row_group, 0)
    for k in range(n_looped, BLOCK_B - 1):
        branches(k, z_refs[k % n_z], in_proj_fills(k + 1, z_refs[(k + 1) % n_z]))

    n_slab = SSM_WIDTH // LANES
    sup_steps = BLOCK_T // S5_STRIDE
    sup_rows = sup_steps * BLOCK_B

    for t in range(BLOCK_T):
        tp, jj = divmod(t, S5_STRIDE)
        for s in range(n_slab):
            lane0 = jj * SSM_WIDTH + s * LANES
            xbp_ref[tp * BLOCK_B:(tp + 1) * BLOCK_B, lane0:lane0 + LANES] = (
                perm_ref[s, pl.ds(t, BLOCK_B, stride=PERM_PITCH), :])

    def g_proj(rows):
        def f():
            hb_ref[BLOCK_B + rows.start:BLOCK_B + rows.stop, :] = _dot(
                xbp_ref[rows, :].astype(BF16), wbs_ref[...])
        return f

    half = sup_rows // 2
    branches(BLOCK_B - 1, z_refs[(BLOCK_B - 1) % n_z],
             (g_proj(slice(0, half)), g_proj(slice(half, sup_rows)), lambda: None))

    lam_r = jnp.broadcast_to(lamr_ref[lrow, :], (BLOCK_B, N_STATE))
    lam_i = jnp.broadcast_to(lami_ref[lrow, :], (BLOCK_B, N_STATE))

    steps = sup_steps // S5_CHUNKS
    chunk = steps * BLOCK_B
    state = [hb_ref[0:BLOCK_B, 0:N_STATE], hb_ref[0:BLOCK_B, N_STATE:]]

    def scan_chunk(c):
        hr, hi = state
        for tp in range(c * steps, (c + 1) * steps):
            rows = slice((tp + 1) * BLOCK_B, (tp + 2) * BLOCK_B)
            gr = hb_ref[rows, 0:N_STATE]
            gi = hb_ref[rows, N_STATE:]
            hr, hi = lam_r * hr - lam_i * hi + gr, lam_r * hi + lam_i * hr + gi
            hb_ref[rows, 0:N_STATE] = hr
            hb_ref[rows, N_STATE:] = hi
        state[0], state[1] = hr, hi

    def c_proj(c):
        r0 = c * chunk
        x_c = xbp_ref[r0:r0 + chunk, :]
        d = d_ref[lrow, :]
        h_prev = hb_ref[r0:r0 + chunk, :].astype(BF16)
        ys = []
        for jj in range(S5_STRIDE - 1):
            k0 = KD_OFFSETS[jj]
            n_in = (jj + 1) * SSM_WIDTH
            ys.append(_dot(h_prev, wcp_ref[jj])
                      + _dot(x_c[:, 0:n_in].astype(BF16), kd_ref[k0:k0 + n_in, :])
                      + d * x_c[:, jj * SSM_WIDTH:(jj + 1) * SSM_WIDTH])
        h_last = hb_ref[BLOCK_B + r0:BLOCK_B + r0 + chunk, :].astype(BF16)
        ys.append(_dot(h_last, wc_ref[...]) + d * x_c[:, (S5_STRIDE - 1) * SSM_WIDTH:])
        return ys

    def glu(c, ys):
        r0 = c * chunk
        for jj, y in enumerate(ys):
            yg = _gelu(y)
            xbp_ref[r0:r0 + chunk, jj * SSM_WIDTH:(jj + 1) * SSM_WIDTH] = (
                yg * _sigmoid(_dot(yg.astype(BF16), gluw_ref[...]) + glub_ref[lrow, :]))

    ys_prev = None
    for c in range(S5_CHUNKS):
        scan_chunk(c)
        ys_c = c_proj(c)
        if ys_prev is not None:
            glu(c - 1, ys_prev)
        ys_prev = ys_c
    glu(S5_CHUNKS - 1, ys_prev)
    hb_ref[0:BLOCK_B, 0:N_STATE] = state[0]
    hb_ref[0:BLOCK_B, N_STATE:] = state[1]

    for t in range(BLOCK_T):
        tp, jj = divmod(t, S5_STRIDE)
        for s in range(n_slab):
            lane0 = jj * SSM_WIDTH + s * LANES
            perm_ref[s, pl.ds(t, BLOCK_B, stride=PERM_PITCH), :] = (
                xbp_ref[tp * BLOCK_B:(tp + 1) * BLOCK_B, lane0:lane0 + LANES])

    for b in range(BLOCK_B):
        ybb = jnp.concatenate(
            [perm_ref[s, b * PERM_PITCH:b * PERM_PITCH + BLOCK_T, :] for s in range(n_slab)],
            axis=1)
        rows = slice(b * BLOCK_T, (b + 1) * BLOCK_T)
        y_ref[rows, Y_B:Y_X] = (ybb * gateb_ref[rows, :]).astype(BF16)

    for b in range(BLOCK_B):
        rows = slice(b * BLOCK_T, (b + 1) * BLOCK_T)
        r = alpha * x_ref[b] + _dot(y_ref[rows, :], w_out_ref[...])
        if b == BLOCK_B - 1:
            for f in in_proj_fills(0, z_refs[0], x_row=xn_ref[0]):
                f()
        mu = jnp.mean(r, axis=-1, keepdims=True)
        rc = r - mu
        var = jnp.mean(rc * rc, axis=-1, keepdims=True)
        o_ref[b] = rc * lax.rsqrt(var + LN_EPS) * lng_ref[lrow, :] + lnb_ref[lrow, :]


def _full(shape):
    nd = len(shape)
    return pl.BlockSpec(shape, lambda i, j, _nd=nd: (0,) * _nd)


def _of_layer(shape, layer):
    nd = len(shape)
    return pl.BlockSpec((None,) + tuple(shape),
                        lambda i, j, _l=layer, _nd=nd: (_l,) + (0,) * _nd)


def _kv_project(mem, wkt16, wv16):
    bsz, m_len, d = mem.shape
    depth = wkt16.shape[0]
    kt_sds = jax.ShapeDtypeStruct((depth, bsz, XA_WIDTH, m_len), BF16)
    v_sds = jax.ShapeDtypeStruct((depth, bsz, m_len, XA_WIDTH), BF16)
    return pl.pallas_call(
        _kv_kernel,
        grid=(bsz // KV_BLOCK_B,),
        in_specs=[pl.BlockSpec((KV_BLOCK_B, m_len, d), lambda i: (i, 0, 0)),
                  pl.BlockSpec((depth, XA_WIDTH, d), lambda i: (0, 0, 0)),
                  pl.BlockSpec((depth, d, XA_WIDTH), lambda i: (0, 0, 0))],
        out_specs=[pl.BlockSpec((depth, KV_BLOCK_B, XA_WIDTH, m_len), lambda i: (0, i, 0, 0)),
                   pl.BlockSpec((depth, KV_BLOCK_B, m_len, XA_WIDTH), lambda i: (0, i, 0, 0))],
        out_shape=[kt_sds, v_sds],
        name="kv_project",
    )(mem, wkt16, wv16)


def _layer(x, layer, kt, vv, big, small, alpha):
    bsz, seq, d = x.shape
    depth, _, _, m_len = kt.shape
    grid = (bsz // BLOCK_B, seq // BLOCK_T)
    n_steps = grid[0] * grid[1]

    def next_row0(i, j):
        s = jnp.minimum(i * grid[1] + j + 1, n_steps - 1)
        return (s // grid[1]) * BLOCK_B, s % grid[1], 0

    in_specs = [
        pl.BlockSpec((BLOCK_B, BLOCK_T, d), lambda i, j: (i, j, 0)),
        pl.BlockSpec((1, BLOCK_T, d), next_row0),
        pl.BlockSpec((None, BLOCK_B, XA_WIDTH, m_len), lambda i, j: (layer, i, 0, 0)),
        pl.BlockSpec((None, BLOCK_B, m_len, XA_WIDTH), lambda i, j: (layer, i, 0, 0)),
        _full((d, IN_WIDTH)), _full((d, d)),
        _full((S5_STRIDE * SSM_WIDTH, 2 * N_STATE)), _full((2 * N_STATE, SSM_WIDTH)),
        _full((S5_STRIDE - 1, 2 * N_STATE, SSM_WIDTH)), _full((KD_ROWS, SSM_WIDTH)),
        _full((SSM_WIDTH, SSM_WIDTH)),
        _of_layer((GM_HEADS, CHUNK, CHUNK), layer),
        _of_layer((CHUNK, GM_WIDTH), layer),
        _full((depth, GM_WIDTH)), _full((depth, GM_WIDTH)),
        _full((depth, N_STATE)), _full((depth, N_STATE)),
        _full((depth, SSM_WIDTH)), _full((depth, SSM_WIDTH)),
        _full((depth, d)), _full((depth, d)),
    ]
    scratch = [
        pltpu.VMEM((BLOCK_T, IN_WIDTH), F32),
        pltpu.VMEM((BLOCK_T, IN_WIDTH), F32),
        pltpu.VMEM((GM_HEADS // 2, CHUNK, 2 * CHUNK), BF16),
        pltpu.VMEM((ROWS, d), BF16),
        pltpu.VMEM((SSM_WIDTH // LANES, BLOCK_B * PERM_PITCH, LANES), F32),
        pltpu.VMEM((ROWS // S5_STRIDE, S5_STRIDE * SSM_WIDTH), F32),
        pltpu.VMEM((ROWS // S5_STRIDE + BLOCK_B, 2 * N_STATE), F32),
        pltpu.VMEM((ROWS, SSM_WIDTH), F32),
    ]
    return pl.pallas_call(
        functools.partial(_layer_kernel, alpha, layer),
        grid=grid,
        in_specs=in_specs,
        out_specs=pl.BlockSpec((BLOCK_B, BLOCK_T, d), lambda i, j: (i, j, 0)),
        out_shape=jax.ShapeDtypeStruct(x.shape, x.dtype),
        scratch_shapes=scratch,
        compiler_params=pltpu.CompilerParams(
            dimension_semantics=("arbitrary", "arbitrary"),
            vmem_limit_bytes=VMEM_LIMIT_BYTES),
        name="trunk_layer",
    )(x, x, kt, vv, big["w_in"], big["w_out"], big["wbs"], big["wc"], big["wcp"], big["kd"],
      big["glu_w"],
      small["gm_w"], small["gm_bias"], small["gm_g"], small["gm_b"], small["lam_r"],
      small["lam_i"], small["d"], small["glu_b"], small["ln_g"], small["ln_b"])


def _ssm_discretize(lam_re, lam_im, log_step, b_re, b_im, c_re, c_im):
    depth = lam_re.shape[0]
    step = jnp.exp(log_step)[:, :, None]
    a, b = lam_re * step, lam_im * step

    def lam_pow(m):
        mag = jnp.exp(m * a)
        return mag * jnp.cos(m * b), mag * jnp.sin(m * b)

    def cmul(xr, xi, yr, yi):
        return xr * yr - xi * yi, xr * yi + xi * yr

    eye = jnp.eye(SSM_GROUPS, dtype=F32)

    def in_blocks(m):
        return jnp.einsum('lgpc,gk->lgckp', m, eye).reshape(depth, SSM_WIDTH, N_STATE)

    def out_blocks(m):
        return jnp.einsum('lgcp,gk->lgpkc', m, eye).reshape(depth, N_STATE, SSM_WIDTH)

    lbr, lbi = lam_pow(1)
    den = lam_re * lam_re + lam_im * lam_im
    fr = ((lbr - 1.0) * lam_re + lbi * lam_im) / den
    fi = (lbi * lam_re - (lbr - 1.0) * lam_im) / den
    bbr, bbi = cmul(fr[..., None], fi[..., None], b_re, b_im)

    wbs = []
    for j in range(S5_STRIDE):
        pr, pi = lam_pow(S5_STRIDE - 1 - j)
        wr, wi = cmul(pr[..., None], pi[..., None], bbr, bbi)
        wbs.append(jnp.concatenate([in_blocks(wr), in_blocks(wi)], axis=2))
    wbs = jnp.concatenate(wbs, axis=1)

    def c_times_lam_pow(m):
        pr, pi = lam_pow(m)
        return cmul(c_re, c_im, pr[:, :, None, :], pi[:, :, None, :])

    def re_c_blocks(cr, ci):
        return jnp.concatenate([out_blocks(cr), out_blocks(-ci)], axis=1)

    wc = re_c_blocks(c_re, c_im)
    wcp = jnp.stack([re_c_blocks(*c_times_lam_pow(j + 1)) for j in range(S5_STRIDE - 1)],
                    axis=1)

    def direct_blocks(m):
        cr, ci = c_times_lam_pow(m)
        k_m = (jnp.einsum('lgop,lgpi->lgio', cr, bbr)
               - jnp.einsum('lgop,lgpi->lgio', ci, bbi))
        return jnp.einsum('lgio,gk->lgiko', k_m, eye).reshape(depth, SSM_WIDTH, SSM_WIDTH)

    k_dir = [direct_blocks(m) for m in range(S5_STRIDE - 1)]
    kd = jnp.concatenate([k_dir[j - i] for j in range(S5_STRIDE - 1) for i in range(j + 1)],
                         axis=1)
    lsr, lsi = lam_pow(S5_STRIDE)
    return lsr.reshape(depth, N_STATE), lsi.reshape(depth, N_STATE), wbs, wc, wcp, kd


def kernel(x, mem, w_in, gm_w_s, gm_b_s, gm_ln_g, gm_ln_b, ssm_lam_re, ssm_lam_im,
           ssm_log_step, ssm_b_re, ssm_b_im, ssm_c_re, ssm_c_im, ssm_d, glu_w, glu_b,
           xa_w_k, xa_w_v, w_out, ln_g, ln_b):
    depth = w_in.shape[0]
    alpha = float((2 * depth) ** 0.25)
    lam_r, lam_i, wbs, wc, wcp, kd = _ssm_discretize(
        ssm_lam_re, ssm_lam_im, ssm_log_step, ssm_b_re, ssm_b_im, ssm_c_re, ssm_c_im)
    small = {
        "gm_w": gm_w_s,
        "gm_bias": jnp.repeat(jnp.swapaxes(gm_b_s, 1, 2), GM_HEAD_DIM, axis=2),
        "gm_g": gm_ln_g.reshape(depth, GM_WIDTH), "gm_b": gm_ln_b.reshape(depth, GM_WIDTH),
        "lam_r": lam_r, "lam_i": lam_i, "d": ssm_d, "glu_b": glu_b,
        "ln_g": ln_g, "ln_b": ln_b,
    }
    w_in16, w_out16 = w_in.astype(BF16), w_out.astype(BF16)
    wbs16, wc16, wcp16, kd16 = (m.astype(BF16) for m in (wbs, wc, wcp, kd))
    glu_w16 = glu_w.astype(BF16)
    kt, vv = _kv_project(mem, jnp.swapaxes(xa_w_k, 1, 2).astype(BF16), xa_w_v.astype(BF16))
    for l in range(depth):
        big = {"w_in": w_in16[l], "w_out": w_out16[l], "wbs": wbs16[l], "wc": wc16[l],
               "wcp": wcp16[l], "kd": kd16[l], "glu_w": glu_w16[l]}
        x = _layer(x, l, kt, vv, big, small, alpha)
    return x
```

```python
import functools
import math

import jax
import jax.numpy as jnp
import numpy as np
from jax import lax
from jax.experimental import pallas as pl
from jax.experimental.pallas import tpu as pltpu

F32 = jnp.float32
BF16 = jnp.bfloat16

D_MODEL = 1024
GM_WIDTH = 512
GM_HEADS = 4
GM_HEAD_DIM = 128
CHUNK = 128
SSM_WIDTH = 256
SSM_GROUP = 16
SSM_GROUPS = 16
SSM_STATE = 64
N_STATE = SSM_GROUPS * SSM_STATE
XA_WIDTH = 256
XA_HEADS = 4
XA_HEAD_DIM = 64
IN_WIDTH = 3 * GM_WIDTH + 2 * SSM_WIDTH + 2 * XA_WIDTH
LN_EPS = 1e-5

C_U, C_V, C_GA = 0, GM_WIDTH, 2 * GM_WIDTH
C_XB = 3 * GM_WIDTH
C_GB = C_XB + SSM_WIDTH
C_Q = C_GB + SSM_WIDTH
C_GX = C_Q + XA_WIDTH
Y_A, Y_B, Y_X = 0, GM_WIDTH, GM_WIDTH + SSM_WIDTH

BLOCK_B = 8
BLOCK_T = CHUNK
ROWS = BLOCK_B * BLOCK_T
ROWS_PER_ITER = 2
S5_CHUNKS = 4
S5_STRIDE = 2
KD_OFFSETS = tuple(SSM_WIDTH * j * (j + 1) // 2 for j in range(S5_STRIDE - 1))
KD_ROWS = SSM_WIDTH * S5_STRIDE * (S5_STRIDE - 1) // 2
KV_BLOCK_B = 4
LANES = 128
PERM_PITCH = BLOCK_T + 8
VMEM_LIMIT_BYTES = 60 * 1024 * 1024

_LOG2E = 1.0 / math.log(2.0)
_GELU_K1 = -2.0 * math.sqrt(2.0 / math.pi) * _LOG2E
_GELU_K3 = _GELU_K1 * 0.044715


def _gelu(x):
    return x * (1.0 / (1.0 + jnp.exp2(x * (_GELU_K1 + _GELU_K3 * (x * x)))))


def _sigmoid(x):
    return 1.0 / (1.0 + jnp.exp2(x * (-_LOG2E)))


def _silu(x):
    return x * _sigmoid(x)


def _dot(a, b):
    return jnp.dot(a, b, preferred_element_type=F32)


def _aligned(v, m):
    return v if isinstance(v, int) else pl.multiple_of(v, m)


def _kv_kernel(mem_ref, wkt_ref, wv_ref, kt_ref, v_ref):
    depth = wkt_ref.shape[0]
    for b in range(KV_BLOCK_B):
        m16 = mem_ref[b].astype(BF16)
        for l in range(depth):
            kt = lax.dot_general(wkt_ref[l], m16, (((1,), (1,)), ((), ())),
                                 preferred_element_type=F32)
            kt_ref[l, b] = kt.astype(BF16)
            v_ref[l, b] = _dot(m16, wv_ref[l]).astype(BF16)


def _layer_kernel(alpha, layer,
                  x_ref, xn_ref, kt_ref, v_ref, w_in_ref, w_out_ref, wbs_ref, wc_ref, wcp_ref,
                  kd_ref, gluw_ref,
                  gmw_ref, gmbias_ref, gmg_ref, gmb_ref, lamr_ref, lami_ref, d_ref,
                  glub_ref, lng_ref, lnb_ref,
                  o_ref,
                  z0_ref, z1_ref, wm_ref, y_ref, perm_ref, xbp_ref, hb_ref, gateb_ref):
    j = pl.program_id(1)
    lrow = slice(layer, layer + 1)
    z_refs = (z0_ref, z1_ref)
    assert len(z_refs) == ROWS_PER_ITER

    @pl.when(j == 0)
    def _():
        hb_ref[0:BLOCK_B, :] = jnp.zeros((BLOCK_B, 2 * N_STATE), F32)

    tri = (lax.broadcasted_iota(jnp.int32, (CHUNK, CHUNK), 0)
           >= lax.broadcasted_iota(jnp.int32, (CHUNK, CHUNK), 1))
    for hp in range(GM_HEADS // 2):
        wm_ref[hp] = jnp.concatenate(
            [jnp.where(tri, gmw_ref[2 * hp + i], 0.0).astype(BF16) for i in range(2)], axis=1)
    row_head = lax.broadcasted_iota(jnp.int32, (XA_WIDTH, XA_WIDTH), 0) // XA_HEAD_DIM
    col_head = lax.broadcasted_iota(jnp.int32, (XA_WIDTH, XA_WIDTH), 1) // XA_HEAD_DIM
    out_head = lax.broadcasted_iota(jnp.int32, (BLOCK_T, XA_WIDTH), 1) // XA_HEAD_DIM

    def in_proj_fills(row, z_ref, x_row=None):
        x16 = (x_ref[row] if x_row is None else x_row).astype(BF16)

        def chunk_uv():
            z_ref[:, C_U:C_GA] = _dot(x16, w_in_ref[:, C_U:C_GA])

        def chunk_ssm():
            zz = _dot(x16, w_in_ref[:, C_GA:C_Q])
            z_ref[:, C_GA:C_XB] = zz[:, 0:GM_WIDTH]
            xs = zz[:, C_XB - C_GA:C_GB - C_GA]
            p0 = _aligned(row * PERM_PITCH, 8)
            for s in range(SSM_WIDTH // LANES):
                perm_ref[s, pl.ds(p0, BLOCK_T), :] = xs[:, s * LANES:(s + 1) * LANES]
            r0 = _aligned(row * BLOCK_T, BLOCK_T)
            gateb_ref[pl.ds(r0, BLOCK_T), :] = _silu(zz[:, C_GB - C_GA:])

        def chunk_attn():
            z_ref[:, C_Q:] = _dot(x16, w_in_ref[:, C_Q:])

        return chunk_uv, chunk_ssm, chunk_attn

    def branches(row, z_ref, fills):
        r0 = _aligned(row * BLOCK_T, BLOCK_T)
        q16 = (z_ref[:, C_Q:C_GX] * (XA_HEAD_DIM ** -0.5 * _LOG2E)).astype(BF16)
        kt = kt_ref[row]
        vv = v_ref[row]
        scores = [_dot(q16, jnp.where(row_head == h, kt, jnp.zeros_like(kt)))
                  for h in range(XA_HEADS)]
        fills[0]()
        fills[1]()
        o = None
        scale = None
        for h in range(XA_HEADS):
            s = scores[h]
            p = jnp.exp2(s - jnp.max(s, axis=-1, keepdims=True))
            inv = 1.0 / jnp.sum(p, axis=-1, keepdims=True)
            oh = _dot(p.astype(BF16), jnp.where(col_head == h, vv, jnp.zeros_like(vv)))
            o = oh if o is None else o + oh
            scale = inv if scale is None else jnp.where(out_head == h, inv, scale)
        v = _gelu(z_ref[:, C_V:C_GA])
        vn16 = []
        for h in range(GM_HEADS):
            sl = slice(h * GM_HEAD_DIM, (h + 1) * GM_HEAD_DIM)
            vh = v[:, sl]
            mu = jnp.mean(vh, axis=-1, keepdims=True)
            xc = vh - mu
            var = jnp.mean(xc * xc, axis=-1, keepdims=True)
            vn = xc * lax.rsqrt(var + LN_EPS) * gmg_ref[lrow, sl] + gmb_ref[lrow, sl]
            vn16.append(vn.astype(BF16))
        mixed = []
        zero = jnp.zeros((CHUNK, GM_HEAD_DIM), BF16)
        for hp in range(GM_HEADS // 2):
            v_a, v_b = vn16[2 * hp], vn16[2 * hp + 1]
            rhs = jnp.concatenate([jnp.concatenate([v_a, zero], axis=1),
                                   jnp.concatenate([zero, v_b], axis=1)], axis=0)
            mixed.append(_dot(wm_ref[hp], rhs))
        fills[2]()
        u = _gelu(z_ref[:, C_U:C_V])
        ya = u * (jnp.concatenate(mixed, axis=1) + gmbias_ref[...]) * _silu(z_ref[:, C_GA:C_XB])
        y_ref[pl.ds(r0, BLOCK_T), Y_A:Y_B] = ya.astype(BF16)
        yx = o * scale * _silu(z_ref[:, C_GX:])
        y_ref[pl.ds(r0, BLOCK_T), Y_X:] = yx.astype(BF16)

    n_z = len(z_refs)

    @pl.when((pl.program_id(0) == 0) & (j == 0))
    def _():
        for f in in_proj_fills(0, z_refs[0]):
            f()

    def row_group(g, carry):
        k0 = ROWS_PER_ITER * g
        for i in range(ROWS_PER_ITER):
            branches(k0 + i, z_refs[i % n_z], in_proj_fills(k0 + i + 1, z_refs[(i + 1) % n_z]))
        return carry

    n_looped = (BLOCK_B - 1) // ROWS_PER_ITER * ROWS_PER_ITER
    lax.fori_loop(0, n_looped // ROWS_PER_ITER, row_group, 0)
    for k in range(n_looped, BLOCK_B - 1):
        branches(k, z_refs[k % n_z], in_proj_fills(k + 1, z_refs[(k + 1) % n_z]))

    n_slab = SSM_WIDTH // LANES
    sup_steps = BLOCK_T // S5_STRIDE
    sup_rows = sup_steps * BLOCK_B

    for t in range(BLOCK_T):
        tp, jj = divmod(t, S5_STRIDE)
        for s in range(n_slab):
            lane0 = jj * SSM_WIDTH + s * LANES
            xbp_ref[tp * BLOCK_B:(tp + 1) * BLOCK_B, lane0:lane0 + LANES] = (
                perm_ref[s, pl.ds(t, BLOCK_B, stride=PERM_PITCH), :])

    def g_proj(rows):
        def f():
            hb_ref[BLOCK_B + rows.start:BLOCK_B + rows.stop, :] = _dot(
                xbp_ref[rows, :].astype(BF16), wbs_ref[...])
        return f

    half = sup_rows // 2
    branches(BLOCK_B - 1, z_refs[(BLOCK_B - 1) % n_z],
             (g_proj(slice(0, half)), g_proj(slice(half, sup_rows)), lambda: None))

    lam_r = jnp.broadcast_to(lamr_ref[lrow, :], (BLOCK_B, N_STATE))
    lam_i = jnp.broadcast_to(lami_ref[lrow, :], (BLOCK_B, N_STATE))

    steps = sup_steps // S5_CHUNKS
    chunk = steps * BLOCK_B
    state = [hb_ref[0:BLOCK_B, 0:N_STATE], hb_ref[0:BLOCK_B, N_STATE:]]

    def scan_chunk(c):
        hr, hi = state
        for tp in range(c * steps, (c + 1) * steps):
            rows = slice((tp + 1) * BLOCK_B, (tp + 2) * BLOCK_B)
            gr = hb_ref[rows, 0:N_STATE]
            gi = hb_ref[rows, N_STATE:]
            hr, hi = lam_r * hr - lam_i * hi + gr, lam_r * hi + lam_i * hr + gi
            hb_ref[rows, 0:N_STATE] = hr
            hb_ref[rows, N_STATE:] = hi
        state[0], state[1] = hr, hi

    def c_proj(c):
        r0 = c * chunk
        x_c = xbp_ref[r0:r0 + chunk, :]
        d = d_ref[lrow, :]
        h_prev = hb_ref[r0:r0 + chunk, :].astype(BF16)
        ys = []
        for jj in range(S5_STRIDE - 1):
            k0 = KD_OFFSETS[jj]
            n_in = (jj + 1) * SSM_WIDTH
            ys.append(_dot(h_prev, wcp_ref[jj])
                      + _dot(x_c[:, 0:n_in].astype(BF16), kd_ref[k0:k0 + n_in, :])
                      + d * x_c[:, jj * SSM_WIDTH:(jj + 1) * SSM_WIDTH])
        h_last = hb_ref[BLOCK_B + r0:BLOCK_B + r0 + chunk, :].astype(BF16)
        ys.append(_dot(h_last, wc_ref[...]) + d * x_c[:, (S5_STRIDE - 1) * SSM_WIDTH:])
        return ys

    def glu(c, ys):
        r0 = c * chunk
        for jj, y in enumerate(ys):
            yg = _gelu(y)
            xbp_ref[r0:r0 + chunk, jj * SSM_WIDTH:(jj + 1) * SSM_WIDTH] = (
                yg * _sigmoid(_dot(yg.astype(BF16), gluw_ref[...]) + glub_ref[lrow, :]))

    ys_prev = None
    for c in range(S5_CHUNKS):
        scan_chunk(c)
        ys_c = c_proj(c)
        if ys_prev is not None:
            glu(c - 1, ys_prev)
        ys_prev = ys_c
    glu(S5_CHUNKS - 1, ys_prev)
    hb_ref[0:BLOCK_B, 0:N_STATE] = state[0]
    hb_ref[0:BLOCK_B, N_STATE:] = state[1]

    for t in range(BLOCK_T):
        tp, jj = divmod(t, S5_STRIDE)
        for s in range(n_slab):
            lane0 = jj * SSM_WIDTH + s * LANES
            perm_ref[s, pl.ds(t, BLOCK_B, stride=PERM_PITCH), :] = (
                xbp_ref[tp * BLOCK_B:(tp + 1) * BLOCK_B, lane0:lane0 + LANES])

    for b in range(BLOCK_B):
        ybb = jnp.concatenate(
            [perm_ref[s, b * PERM_PITCH:b * PERM_PITCH + BLOCK_T, :] for s in range(n_slab)],
            axis=1)
        rows = slice(b * BLOCK_T, (b + 1) * BLOCK_T)
        y_ref[rows, Y_B:Y_X] = (ybb * gateb_ref[rows, :]).astype(BF16)

    for b in range(BLOCK_B):
        rows = slice(b * BLOCK_T, (b + 1) * BLOCK_T)
        r = alpha * x_ref[b] + _dot(y_ref[rows, :], w_out_ref[...])
        if b == BLOCK_B - 1:
            for f in in_proj_fills(0, z_refs[0], x_row=xn_ref[0]):
                f()
        mu = jnp.mean(r, axis=-1, keepdims=True)
        rc = r - mu
        var = jnp.mean(rc * rc, axis=-1, keepdims=True)
        o_ref[b] = rc * lax.rsqrt(var + LN_EPS) * lng_ref[lrow, :] + lnb_ref[lrow, :]


def _full(shape):
    nd = len(shape)
    return pl.BlockSpec(shape, lambda i, j, _nd=nd: (0,) * _nd)


def _of_layer(shape, layer):
    nd = len(shape)
    return pl.BlockSpec((None,) + tuple(shape),
                        lambda i, j, _l=layer, _nd=nd: (_l,) + (0,) * _nd)


def _kv_project(mem, wkt16, wv16):
    bsz, m_len, d = mem.shape
    depth = wkt16.shape[0]
    kt_sds = jax.ShapeDtypeStruct((depth, bsz, XA_WIDTH, m_len), BF16)
    v_sds = jax.ShapeDtypeStruct((depth, bsz, m_len, XA_WIDTH), BF16)
    return pl.pallas_call(
        _kv_kernel,
        grid=(bsz // KV_BLOCK_B,),
        in_specs=[pl.BlockSpec((KV_BLOCK_B, m_len, d), lambda i: (i, 0, 0)),
                  pl.BlockSpec((depth, XA_WIDTH, d), lambda i: (0, 0, 0)),
                  pl.BlockSpec((depth, d, XA_WIDTH), lambda i: (0, 0, 0))],
        out_specs=[pl.BlockSpec((depth, KV_BLOCK_B, XA_WIDTH, m_len), lambda i: (0, i, 0, 0)),
                   pl.BlockSpec((depth, KV_BLOCK_B, m_len, XA_WIDTH), lambda i: (0, i, 0, 0))],
        out_shape=[kt_sds, v_sds],
        name="kv_project",
    )(mem, wkt16, wv16)


def _layer(x, layer, kt, vv, big, small, alpha):
    bsz, seq, d = x.shape
    depth, _, _, m_len = kt.shape
    grid = (bsz // BLOCK_B, seq // BLOCK_T)
    n_steps = grid[0] * grid[1]

    def next_row0(i, j):
        s = jnp.minimum(i * grid[1] + j + 1, n_steps - 1)
        return (s // grid[1]) * BLOCK_B, s % grid[1], 0

    in_specs = [
        pl.BlockSpec((BLOCK_B, BLOCK_T, d), lambda i, j: (i, j, 0)),
        pl.BlockSpec((1, BLOCK_T, d), next_row0),
        pl.BlockSpec((None, BLOCK_B, XA_WIDTH, m_len), lambda i, j: (layer, i, 0, 0)),
        pl.BlockSpec((None, BLOCK_B, m_len, XA_WIDTH), lambda i, j: (layer, i, 0, 0)),
        _full((d, IN_WIDTH)), _full((d, d)),
        _full((S5_STRIDE * SSM_WIDTH, 2 * N_STATE)), _full((2 * N_STATE, SSM_WIDTH)),
        _full((S5_STRIDE - 1, 2 * N_STATE, SSM_WIDTH)), _full((KD_ROWS, SSM_WIDTH)),
        _full((SSM_WIDTH, SSM_WIDTH)),
        _of_layer((GM_HEADS, CHUNK, CHUNK), layer),
        _of_layer((CHUNK, GM_WIDTH), layer),
        _full((depth, GM_WIDTH)), _full((depth, GM_WIDTH)),
        _full((depth, N_STATE)), _full((depth, N_STATE)),
        _full((depth, SSM_WIDTH)), _full((depth, SSM_WIDTH)),
        _full((depth, d)), _full((depth, d)),
    ]
    scratch = [
        pltpu.VMEM((BLOCK_T, IN_WIDTH), F32),
        pltpu.VMEM((BLOCK_T, IN_WIDTH), F32),
        pltpu.VMEM((GM_HEADS // 2, CHUNK, 2 * CHUNK), BF16),
        pltpu.VMEM((ROWS, d), BF16),
        pltpu.VMEM((SSM_WIDTH // LANES, BLOCK_B * PERM_PITCH, LANES), F32),
        pltpu.VMEM((ROWS // S5_STRIDE, S5_STRIDE * SSM_WIDTH), F32),
        pltpu.VMEM((ROWS // S5_STRIDE + BLOCK_B, 2 * N_STATE), F32),
        pltpu.VMEM((ROWS, SSM_WIDTH), F32),
    ]
    return pl.pallas_call(
        functools.partial(_layer_kernel, alpha, layer),
        grid=grid,
        in_specs=in_specs,
        out_specs=pl.BlockSpec((BLOCK_B, BLOCK_T, d), lambda i, j: (i, j, 0)),
        out_shape=jax.ShapeDtypeStruct(x.shape, x.dtype),
        scratch_shapes=scratch,
        compiler_params=pltpu.CompilerParams(
            dimension_semantics=("arbitrary", "arbitrary"),
            vmem_limit_bytes=VMEM_LIMIT_BYTES),
        name="trunk_layer",
    )(x, x, kt, vv, big["w_in"], big["w_out"], big["wbs"], big["wc"], big["wcp"], big["kd"],
      big["glu_w"],
      small["gm_w"], small["gm_bias"], small["gm_g"], small["gm_b"], small["lam_r"],
      small["lam_i"], small["d"], small["glu_b"], small["ln_g"], small["ln_b"])


def _ssm_discretize(lam_re, lam_im, log_step, b_re, b_im, c_re, c_im):
    depth = lam_re.shape[0]
    step = jnp.exp(log_step)[:, :, None]
    a, b = lam_re * step, lam_im * step

    def lam_pow(m):
        mag = jnp.exp(m * a)
        return mag * jnp.cos(m * b), mag * jnp.sin(m * b)

    def cmul(xr, xi, yr, yi):
        return xr * yr - xi * yi, xr * yi + xi * yr

    def group_block_diag(m, per_row, per_col):
        rows, cols = SSM_GROUPS * per_row, SSM_GROUPS * per_col
        spread = (np.arange(per_col)[:, None] == np.arange(cols)[None, :] % per_col)
        tiled = jnp.einsum('...p,pq->...q', m, spread.astype(np.float32),
                           precision=lax.Precision.HIGHEST)
        same_group = (np.arange(rows)[:, None] // per_row == np.arange(cols)[None, :] // per_col)
        return jnp.where(same_group, tiled, 0.0)

    lbr, lbi = lam_pow(1)
    den = lam_re * lam_re + lam_im * lam_im
    fr = ((lbr - 1.0) * lam_re + lbi * lam_im) / den
    fi = (lbi * lam_re - (lbr - 1.0) * lam_im) / den
    bbr, bbi = cmul(fr[..., None], fi[..., None], b_re, b_im)

    coef = [cmul(*(q[..., None] for q in lam_pow(S5_STRIDE - 1 - j)), bbr, bbi)
            for j in range(S5_STRIDE)]
    w_in_side = jnp.stack([jnp.stack(c, axis=1) for c in coef], axis=1)
    w_in_side = jnp.transpose(w_in_side, (0, 1, 2, 3, 5, 4)).reshape(
        depth, S5_STRIDE, 2, SSM_WIDTH, SSM_STATE)
    wbs = group_block_diag(w_in_side, SSM_GROUP, SSM_STATE)
    wbs = jnp.transpose(wbs, (0, 1, 3, 2, 4)).reshape(
        depth, S5_STRIDE * SSM_WIDTH, 2 * N_STATE)

    def c_times_lam_pow(m):
        pr, pi = lam_pow(m)
        return cmul(c_re, c_im, pr[:, :, None, :], pi[:, :, None, :])

    c_pows = [c_times_lam_pow(m) for m in range(S5_STRIDE)]
    w_out_side = jnp.stack([jnp.stack([cr, -ci], axis=1) for cr, ci in c_pows], axis=1)
    w_out_side = jnp.transpose(w_out_side, (0, 1, 2, 3, 5, 4)).reshape(
        depth, S5_STRIDE, 2, N_STATE, SSM_GROUP)
    wc_all = group_block_diag(w_out_side, SSM_STATE, SSM_GROUP).reshape(
        depth, S5_STRIDE, 2 * N_STATE, SSM_WIDTH)
    wc, wcp = wc_all[:, 0], wc_all[:, 1:]

    k_dir = jnp.stack([jnp.einsum('lgop,lgpi->lgio', cr, bbr)
                       - jnp.einsum('lgop,lgpi->lgio', ci, bbi)
                       for cr, ci in c_pows[:S5_STRIDE - 1]], axis=1)
    k_dir = group_block_diag(
        k_dir.reshape(depth, S5_STRIDE - 1, SSM_WIDTH, SSM_GROUP), SSM_GROUP, SSM_GROUP)
    kd = jnp.concatenate([k_dir[:, j - i] for j in range(S5_STRIDE - 1) for i in range(j + 1)],
                         axis=1)
    lsr, lsi = lam_pow(S5_STRIDE)
    return lsr.reshape(depth, N_STATE), lsi.reshape(depth, N_STATE), wbs, wc, wcp, kd


def kernel(x, mem, w_in, gm_w_s, gm_b_s, gm_ln_g, gm_ln_b, ssm_lam_re, ssm_lam_im,
           ssm_log_step, ssm_b_re, ssm_b_im, ssm_c_re, ssm_c_im, ssm_d, glu_w, glu_b,
           xa_w_k, xa_w_v, w_out, ln_g, ln_b):
    depth = w_in.shape[0]
    alpha = float((2 * depth) ** 0.25)
    lam_r, lam_i, wbs, wc, wcp, kd = _ssm_discretize(
        ssm_lam_re, ssm_lam_im, ssm_log_step, ssm_b_re, ssm_b_im, ssm_c_re, ssm_c_im)
    small = {
        "gm_w": gm_w_s,
        "gm_bias": jnp.repeat(jnp.swapaxes(gm_b_s, 1, 2), GM_HEAD_DIM, axis=2),
        "gm_g": gm_ln_g.reshape(depth, GM_WIDTH), "gm_b": gm_ln_b.reshape(depth, GM_WIDTH),
        "lam_r": lam_r, "lam_i": lam_i, "d": ssm_d, "glu_b": glu_b,
        "ln_g": ln_g, "ln_b": ln_b,
    }
    w_in16, w_out16 = w_in.astype(BF16), w_out.astype(BF16)
    wbs16, wc16, wcp16, kd16 = (m.astype(BF16) for m in (wbs, wc, wcp, kd))
    glu_w16 = glu_w.astype(BF16)
    kt, vv = _kv_project(mem, jnp.swapaxes(xa_w_k, 1, 2).astype(BF16), xa_w_v.astype(BF16))
    for l in range(depth):
        big = {"w_in": w_in16[l], "w_out": w_out16[l], "wbs": wbs16[l], "wc": wc16[l],
               "wcp": wcp16[l], "kd": kd16[l], "glu_w": glu_w16[l]}
        x = _layer(x, l, kt, vv, big, small, alpha)
    return x
```

```python
import functools
import math

import jax
import jax.numpy as jnp
import numpy as np
from jax import lax
from jax.experimental import pallas as pl
from jax.experimental.pallas import tpu as pltpu

F32 = jnp.float32
BF16 = jnp.bfloat16

D_MODEL = 1024
GM_WIDTH = 512
GM_HEADS = 4
GM_HEAD_DIM = 128
CHUNK = 128
SSM_WIDTH = 256
SSM_GROUP = 16
SSM_GROUPS = 16
SSM_STATE = 64
N_STATE = SSM_GROUPS * SSM_STATE
XA_WIDTH = 256
XA_HEADS = 4
XA_HEAD_DIM = 64
IN_WIDTH = 3 * GM_WIDTH + 2 * SSM_WIDTH + 2 * XA_WIDTH
LN_EPS = 1e-5

C_U, C_V, C_GA = 0, GM_WIDTH, 2 * GM_WIDTH
C_XB = 3 * GM_WIDTH
C_GB = C_XB + SSM_WIDTH
C_Q = C_GB + SSM_WIDTH
C_GX = C_Q + XA_WIDTH
Y_A, Y_B, Y_X = 0, GM_WIDTH, GM_WIDTH + SSM_WIDTH

BLOCK_B = 8
BLOCK_T = CHUNK
ROWS = BLOCK_B * BLOCK_T
ROWS_PER_ITER = 2
S5_CHUNKS = 4
S5_STRIDE = 2
KD_OFFSETS = tuple(SSM_WIDTH * j * (j + 1) // 2 for j in range(S5_STRIDE - 1))
KD_ROWS = SSM_WIDTH * S5_STRIDE * (S5_STRIDE - 1) // 2
KV_BLOCK_B = 4
LANES = 128
PERM_PITCH = BLOCK_T + 8
VMEM_LIMIT_BYTES = 60 * 1024 * 1024

_LOG2E = 1.0 / math.log(2.0)
_GELU_K1 = -2.0 * math.sqrt(2.0 / math.pi) * _LOG2E
_GELU_K3 = _GELU_K1 * 0.044715


def _gelu(x):
    return x * (1.0 / (1.0 + jnp.exp2(x * (_GELU_K1 + _GELU_K3 * (x * x)))))


def _sigmoid(x):
    return 1.0 / (1.0 + jnp.exp2(x * (-_LOG2E)))


def _silu(x):
    return x * _sigmoid(x)


def _dot(a, b):
    return jnp.dot(a, b, preferred_element_type=F32)


def _aligned(v, m):
    return v if isinstance(v, int) else pl.multiple_of(v, m)


def _kv_kernel(mem_ref, wkt_ref, wv_ref, kt_ref, v_ref):
    depth = wkt_ref.shape[0]
    for b in range(KV_BLOCK_B):
        m16 = mem_ref[b].astype(BF16)
        for l in range(depth):
            kt = lax.dot_general(wkt_ref[l], m16, (((1,), (1,)), ((), ())),
                                 preferred_element_type=F32)
            kt_ref[l, b] = kt.astype(BF16)
            v_ref[l, b] = _dot(m16, wv_ref[l]).astype(BF16)


def _layer_kernel(alpha, layer,
                  x_ref, xn_ref, kt_ref, v_ref, w_in_ref, w_out_ref, wbs_ref, wc_ref, wcp_ref,
                  kd_ref, gluw_ref,
                  gmw_ref, gmbias_ref, gmg_ref, gmb_ref, lamr_ref, lami_ref, d_ref,
                  glub_ref, lng_ref, lnb_ref,
                  o_ref,
                  z0_ref, z1_ref, wm_ref, y_ref, perm_ref, xbp_ref, hb_ref, gateb_ref):
    j = pl.program_id(1)
    lrow = slice(layer, layer + 1)
    z_refs = (z0_ref, z1_ref)
    assert len(z_refs) == ROWS_PER_ITER

    @pl.when(j == 0)
    def _():
        hb_ref[0:BLOCK_B, :] = jnp.zeros((BLOCK_B, 2 * N_STATE), F32)

    tri = (lax.broadcasted_iota(jnp.int32, (CHUNK, CHUNK), 0)
           >= lax.broadcasted_iota(jnp.int32, (CHUNK, CHUNK), 1))
    for hp in range(GM_HEADS // 2):
        wm_ref[hp] = jnp.concatenate(
            [jnp.where(tri, gmw_ref[2 * hp + i], 0.0).astype(BF16) for i in range(2)], axis=1)
    row_head = lax.broadcasted_iota(jnp.int32, (XA_WIDTH, XA_WIDTH), 0) // XA_HEAD_DIM
    col_head = lax.broadcasted_iota(jnp.int32, (XA_WIDTH, XA_WIDTH), 1) // XA_HEAD_DIM
    out_head = lax.broadcasted_iota(jnp.int32, (BLOCK_T, XA_WIDTH), 1) // XA_HEAD_DIM

    def in_proj_fills(row, z_ref, x_row=None):
        x16 = (x_ref[row] if x_row is None else x_row).astype(BF16)

        def chunk_uv():
            z_ref[:, C_U:C_GA] = _dot(x16, w_in_ref[:, C_U:C_GA])

        def chunk_ssm():
            zz = _dot(x16, w_in_ref[:, C_GA:C_Q])
            z_ref[:, C_GA:C_XB] = zz[:, 0:GM_WIDTH]
            xs = zz[:, C_XB - C_GA:C_GB - C_GA]
            p0 = _aligned(row * PERM_PITCH, 8)
            for s in range(SSM_WIDTH // LANES):
                perm_ref[s, pl.ds(p0, BLOCK_T), :] = xs[:, s * LANES:(s + 1) * LANES]
            r0 = _aligned(row * BLOCK_T, BLOCK_T)
            gateb_ref[pl.ds(r0, BLOCK_T), :] = _silu(zz[:, C_GB - C_GA:])

        def chunk_attn():
            z_ref[:, C_Q:] = _dot(x16, w_in_ref[:, C_Q:])

        return chunk_uv, chunk_ssm, chunk_attn

    def branches(row, z_ref, fills):
        r0 = _aligned(row * BLOCK_T, BLOCK_T)
        q16 = (z_ref[:, C_Q:C_GX] * (XA_HEAD_DIM ** -0.5 * _LOG2E)).astype(BF16)
        kt = kt_ref[row]
        vv = v_ref[row]
        scores = [_dot(q16, jnp.where(row_head == h, kt, jnp.zeros_like(kt)))
                  for h in range(XA_HEADS)]
        fills[0]()
        fills[1]()
        o = None
        scale = None
        for h in range(XA_HEADS):
            s = scores[h]
            p = jnp.exp2(s - jnp.max(s, axis=-1, keepdims=True))
            inv = 1.0 / jnp.sum(p, axis=-1, keepdims=True)
            oh = _dot(p.astype(BF16), jnp.where(col_head == h, vv, jnp.zeros_like(vv)))
            o = oh if o is None else o + oh
            scale = inv if scale is None else jnp.where(out_head == h, inv, scale)
        v = _gelu(z_ref[:, C_V:C_GA])
        vn16 = []
        for h in range(GM_HEADS):
            sl = slice(h * GM_HEAD_DIM, (h + 1) * GM_HEAD_DIM)
            vh = v[:, sl]
            mu = jnp.mean(vh, axis=-1, keepdims=True)
            xc = vh - mu
            var = jnp.mean(xc * xc, axis=-1, keepdims=True)
            vn = xc * lax.rsqrt(var + LN_EPS) * gmg_ref[lrow, sl] + gmb_ref[lrow, sl]
            vn16.append(vn.astype(BF16))
        mixed = []
        zero = jnp.zeros((CHUNK, GM_HEAD_DIM), BF16)
        for hp in range(GM_HEADS // 2):
            v_a, v_b = vn16[2 * hp], vn16[2 * hp + 1]
            rhs = jnp.concatenate([jnp.concatenate([v_a, zero], axis=1),
                                   jnp.concatenate([zero, v_b], axis=1)], axis=0)
            mixed.append(_dot(wm_ref[hp], rhs))
        fills[2]()
        u = _gelu(z_ref[:, C_U:C_V])
        ya = u * (jnp.concatenate(mixed, axis=1) + gmbias_ref[...]) * _silu(z_ref[:, C_GA:C_XB])
        y_ref[pl.ds(r0, BLOCK_T), Y_A:Y_B] = ya.astype(BF16)
        yx = o * scale * _silu(z_ref[:, C_GX:])
        y_ref[pl.ds(r0, BLOCK_T), Y_X:] = yx.astype(BF16)

    n_z = len(z_refs)

    @pl.when((pl.program_id(0) == 0) & (j == 0))
    def _():
        for f in in_proj_fills(0, z_refs[0]):
            f()

    def row_group(g, carry):
        k0 = ROWS_PER_ITER * g
        for i in range(ROWS_PER_ITER):
            branches(k0 + i, z_refs[i % n_z], in_proj_fills(k0 + i + 1, z_refs[(i + 1) % n_z]))
        return carry

    n_looped = (BLOCK_B - 1) // ROWS_PER_ITER * ROWS_PER_ITER
    lax.fori_loop(0, n_looped // ROWS_PER_ITER, row_group, 0)
    for k in range(n_looped, BLOCK_B - 1):
        branches(k, z_refs[k % n_z], in_proj_fills(k + 1, z_refs[(k + 1) % n_z]))

    n_slab = SSM_WIDTH // LANES
    sup_steps = BLOCK_T // S5_STRIDE
    sup_rows = sup_steps * BLOCK_B

    for t in range(BLOCK_T):
        tp, jj = divmod(t, S5_STRIDE)
        for s in range(n_slab):
            lane0 = jj * SSM_WIDTH + s * LANES
            xbp_ref[tp * BLOCK_B:(tp + 1) * BLOCK_B, lane0:lane0 + LANES] = (
                perm_ref[s, pl.ds(t, BLOCK_B, stride=PERM_PITCH), :])

    def g_proj(ri):
        def f():
            hb_ref[BLOCK_B:BLOCK_B + sup_rows, ri * N_STATE:(ri + 1) * N_STATE] = _dot(
                xbp_ref[...].astype(BF16), wbs_ref[ri])
        return f

    branches(BLOCK_B - 1, z_refs[(BLOCK_B - 1) % n_z], (g_proj(0), g_proj(1), lambda: None))

    lam_r = jnp.broadcast_to(lamr_ref[lrow, :], (BLOCK_B, N_STATE))
    lam_i = jnp.broadcast_to(lami_ref[lrow, :], (BLOCK_B, N_STATE))

    steps = sup_steps // S5_CHUNKS
    chunk = steps * BLOCK_B
    state = [hb_ref[0:BLOCK_B, 0:N_STATE], hb_ref[0:BLOCK_B, N_STATE:]]

    def scan_chunk(c):
        hr, hi = state
        for tp in range(c * steps, (c + 1) * steps):
            rows = slice((tp + 1) * BLOCK_B, (tp + 2) * BLOCK_B)
            gr = hb_ref[rows, 0:N_STATE]
            gi = hb_ref[rows, N_STATE:]
            hr, hi = lam_r * hr - lam_i * hi + gr, lam_r * hi + lam_i * hr + gi
            hb_ref[rows, 0:N_STATE] = hr
            hb_ref[rows, N_STATE:] = hi
        state[0], state[1] = hr, hi

    def c_proj(c):
        r0 = c * chunk
        x_c = xbp_ref[r0:r0 + chunk, :]
        d = d_ref[lrow, :]
        h_prev = hb_ref[r0:r0 + chunk, :].astype(BF16)
        ys = []
        for jj in range(S5_STRIDE - 1):
            k0 = KD_OFFSETS[jj]
            n_in = (jj + 1) * SSM_WIDTH
            ys.append(_dot(h_prev, wcp_ref[jj])
                      + _dot(x_c[:, 0:n_in].astype(BF16), kd_ref[k0:k0 + n_in, :])
                      + d * x_c[:, jj * SSM_WIDTH:(jj + 1) * SSM_WIDTH])
        h_last = hb_ref[BLOCK_B + r0:BLOCK_B + r0 + chunk, :].astype(BF16)
        ys.append(_dot(h_last, wc_ref[...]) + d * x_c[:, (S5_STRIDE - 1) * SSM_WIDTH:])
        return ys

    def glu(c, ys):
        r0 = c * chunk
        for jj, y in enumerate(ys):
            yg = _gelu(y)
            xbp_ref[r0:r0 + chunk, jj * SSM_WIDTH:(jj + 1) * SSM_WIDTH] = (
                yg * _sigmoid(_dot(yg.astype(BF16), gluw_ref[...]) + glub_ref[lrow, :]))

    ys_prev = None
    for c in range(S5_CHUNKS):
        scan_chunk(c)
        ys_c = c_proj(c)
        if ys_prev is not None:
            glu(c - 1, ys_prev)
        ys_prev = ys_c
    glu(S5_CHUNKS - 1, ys_prev)
    hb_ref[0:BLOCK_B, 0:N_STATE] = state[0]
    hb_ref[0:BLOCK_B, N_STATE:] = state[1]

    for t in range(BLOCK_T):
        tp, jj = divmod(t, S5_STRIDE)
        for s in range(n_slab):
            lane0 = jj * SSM_WIDTH + s * LANES
            perm_ref[s, pl.ds(t, BLOCK_B, stride=PERM_PITCH), :] = (
                xbp_ref[tp * BLOCK_B:(tp + 1) * BLOCK_B, lane0:lane0 + LANES])

    for b in range(BLOCK_B):
        ybb = jnp.concatenate(
            [perm_ref[s, b * PERM_PITCH:b * PERM_PITCH + BLOCK_T, :] for s in range(n_slab)],
            axis=1)
        rows = slice(b * BLOCK_T, (b + 1) * BLOCK_T)
        y_ref[rows, Y_B:Y_X] = (ybb * gateb_ref[rows, :]).astype(BF16)

    for b in range(BLOCK_B):
        rows = slice(b * BLOCK_T, (b + 1) * BLOCK_T)
        r = alpha * x_ref[b] + _dot(y_ref[rows, :], w_out_ref[...])
        if b == BLOCK_B - 1:
            for f in in_proj_fills(0, z_refs[0], x_row=xn_ref[0]):
                f()
        mu = jnp.mean(r, axis=-1, keepdims=True)
        rc = r - mu
        var = jnp.mean(rc * rc, axis=-1, keepdims=True)
        o_ref[b] = rc * lax.rsqrt(var + LN_EPS) * lng_ref[lrow, :] + lnb_ref[lrow, :]


def _full(shape):
    nd = len(shape)
    return pl.BlockSpec(shape, lambda i, j, _nd=nd: (0,) * _nd)


def _of_layer(shape, layer):
    nd = len(shape)
    return pl.BlockSpec((None,) + tuple(shape),
                        lambda i, j, _l=layer, _nd=nd: (_l,) + (0,) * _nd)


def _kv_project(mem, wkt16, wv16):
    bsz, m_len, d = mem.shape
    depth = wkt16.shape[0]
    kt_sds = jax.ShapeDtypeStruct((depth, bsz, XA_WIDTH, m_len), BF16)
    v_sds = jax.ShapeDtypeStruct((depth, bsz, m_len, XA_WIDTH), BF16)
    return pl.pallas_call(
        _kv_kernel,
        grid=(bsz // KV_BLOCK_B,),
        in_specs=[pl.BlockSpec((KV_BLOCK_B, m_len, d), lambda i: (i, 0, 0)),
                  pl.BlockSpec((depth, XA_WIDTH, d), lambda i: (0, 0, 0)),
                  pl.BlockSpec((depth, d, XA_WIDTH), lambda i: (0, 0, 0))],
        out_specs=[pl.BlockSpec((depth, KV_BLOCK_B, XA_WIDTH, m_len), lambda i: (0, i, 0, 0)),
                   pl.BlockSpec((depth, KV_BLOCK_B, m_len, XA_WIDTH), lambda i: (0, i, 0, 0))],
        out_shape=[kt_sds, v_sds],
        name="kv_project",
    )(mem, wkt16, wv16)


def _layer(x, layer, kt, vv, big, small, alpha):
    bsz, seq, d = x.shape
    depth, _, _, m_len = kt.shape
    grid = (bsz // BLOCK_B, seq // BLOCK_T)
    n_steps = grid[0] * grid[1]

    def next_row0(i, j):
        s = jnp.minimum(i * grid[1] + j + 1, n_steps - 1)
        return (s // grid[1]) * BLOCK_B, s % grid[1], 0

    in_specs = [
        pl.BlockSpec((BLOCK_B, BLOCK_T, d), lambda i, j: (i, j, 0)),
        pl.BlockSpec((1, BLOCK_T, d), next_row0),
        pl.BlockSpec((None, BLOCK_B, XA_WIDTH, m_len), lambda i, j: (layer, i, 0, 0)),
        pl.BlockSpec((None, BLOCK_B, m_len, XA_WIDTH), lambda i, j: (layer, i, 0, 0)),
        _full((d, IN_WIDTH)), _full((d, d)),
        _full((2, S5_STRIDE * SSM_WIDTH, N_STATE)), _full((2 * N_STATE, SSM_WIDTH)),
        _full((S5_STRIDE - 1, 2 * N_STATE, SSM_WIDTH)), _full((KD_ROWS, SSM_WIDTH)),
        _full((SSM_WIDTH, SSM_WIDTH)),
        _of_layer((GM_HEADS, CHUNK, CHUNK), layer),
        _of_layer((CHUNK, GM_WIDTH), layer),
        _full((depth, GM_WIDTH)), _full((depth, GM_WIDTH)),
        _full((depth, N_STATE)), _full((depth, N_STATE)),
        _full((depth, SSM_WIDTH)), _full((depth, SSM_WIDTH)),
        _full((depth, d)), _full((depth, d)),
    ]
    scratch = [
        pltpu.VMEM((BLOCK_T, IN_WIDTH), F32),
        pltpu.VMEM((BLOCK_T, IN_WIDTH), F32),
        pltpu.VMEM((GM_HEADS // 2, CHUNK, 2 * CHUNK), BF16),
        pltpu.VMEM((ROWS, d), BF16),
        pltpu.VMEM((SSM_WIDTH // LANES, BLOCK_B * PERM_PITCH, LANES), F32),
        pltpu.VMEM((ROWS // S5_STRIDE, S5_STRIDE * SSM_WIDTH), F32),
        pltpu.VMEM((ROWS // S5_STRIDE + BLOCK_B, 2 * N_STATE), F32),
        pltpu.VMEM((ROWS, SSM_WIDTH), F32),
    ]
    return pl.pallas_call(
        functools.partial(_layer_kernel, alpha, layer),
        grid=grid,
        in_specs=in_specs,
        out_specs=pl.BlockSpec((BLOCK_B, BLOCK_T, d), lambda i, j: (i, j, 0)),
        out_shape=jax.ShapeDtypeStruct(x.shape, x.dtype),
        scratch_shapes=scratch,
        compiler_params=pltpu.CompilerParams(
            dimension_semantics=("arbitrary", "arbitrary"),
            vmem_limit_bytes=VMEM_LIMIT_BYTES),
        name="trunk_layer",
    )(x, x, kt, vv, big["w_in"], big["w_out"], big["wbs"], big["wc"], big["wcp"], big["kd"],
      big["glu_w"],
      small["gm_w"], small["gm_bias"], small["gm_g"], small["gm_b"], small["lam_r"],
      small["lam_i"], small["d"], small["glu_b"], small["ln_g"], small["ln_b"])


def _ssm_discretize(lam_re, lam_im, log_step, b_re, b_im, c_re, c_im):
    depth = lam_re.shape[0]
    step = jnp.exp(log_step)[:, :, None]
    a, b = lam_re * step, lam_im * step

    def lam_pow(m):
        mag = jnp.exp(m * a)
        return mag * jnp.cos(m * b), mag * jnp.sin(m * b)

    def cmul(xr, xi, yr, yi):
        return xr * yr - xi * yi, xr * yi + xi * yr

    def group_block_diag(m, per_row, per_col):
        rows, cols = m.shape[-2], SSM_GROUPS * per_col
        spread = (np.arange(per_col)[:, None] == np.arange(cols)[None, :] % per_col)
        tiled = jnp.einsum('...p,pq->...q', m, spread.astype(np.float32),
                           precision=lax.Precision.HIGHEST, preferred_element_type=F32)
        same_group = ((np.arange(rows)[:, None] // per_row) % SSM_GROUPS
                      == np.arange(cols)[None, :] // per_col)
        return jnp.where(same_group, tiled.astype(BF16), jnp.zeros((), BF16))

    lbr, lbi = lam_pow(1)
    den = lam_re * lam_re + lam_im * lam_im
    fr = ((lbr - 1.0) * lam_re + lbi * lam_im) / den
    fi = (lbi * lam_re - (lbr - 1.0) * lam_im) / den
    bbr, bbi = cmul(fr[..., None], fi[..., None], b_re, b_im)

    coef = [cmul(*(q[..., None] for q in lam_pow(S5_STRIDE - 1 - j)), bbr, bbi)
            for j in range(S5_STRIDE)]
    w_in_side = jnp.stack([jnp.stack([c[ri] for c in coef], axis=1) for ri in range(2)],
                          axis=1)
    w_in_side = jnp.transpose(w_in_side, (0, 1, 2, 3, 5, 4)).reshape(
        depth, 2, S5_STRIDE * SSM_WIDTH, SSM_STATE)
    wbs = group_block_diag(w_in_side, SSM_GROUP, SSM_STATE)

    def c_times_lam_pow(m):
        pr, pi = lam_pow(m)
        return cmul(c_re, c_im, pr[:, :, None, :], pi[:, :, None, :])

    c_pows = [c_times_lam_pow(m) for m in range(S5_STRIDE)]
    w_out_side = jnp.stack([jnp.stack([cr, -ci], axis=1) for cr, ci in c_pows], axis=1)
    w_out_side = jnp.transpose(w_out_side, (0, 1, 2, 3, 5, 4)).reshape(
        depth, S5_STRIDE, 2, N_STATE, SSM_GROUP)
    wc_all = group_block_diag(w_out_side, SSM_STATE, SSM_GROUP).reshape(
        depth, S5_STRIDE, 2 * N_STATE, SSM_WIDTH)
    wc, wcp = wc_all[:, 0], wc_all[:, 1:]

    k_dir = jnp.stack([jnp.einsum('lgop,lgpi->lgio', cr, bbr)
                       - jnp.einsum('lgop,lgpi->lgio', ci, bbi)
                       for cr, ci in c_pows[:S5_STRIDE - 1]], axis=1)
    k_dir = group_block_diag(
        k_dir.reshape(depth, S5_STRIDE - 1, SSM_WIDTH, SSM_GROUP), SSM_GROUP, SSM_GROUP)
    kd = jnp.concatenate([k_dir[:, j - i] for j in range(S5_STRIDE - 1) for i in range(j + 1)],
                         axis=1)
    lsr, lsi = lam_pow(S5_STRIDE)
    return lsr.reshape(depth, N_STATE), lsi.reshape(depth, N_STATE), wbs, wc, wcp, kd


def kernel(x, mem, w_in, gm_w_s, gm_b_s, gm_ln_g, gm_ln_b, ssm_lam_re, ssm_lam_im,
           ssm_log_step, ssm_b_re, ssm_b_im, ssm_c_re, ssm_c_im, ssm_d, glu_w, glu_b,
           xa_w_k, xa_w_v, w_out, ln_g, ln_b):
    depth = w_in.shape[0]
    alpha = float((2 * depth) ** 0.25)
    lam_r, lam_i, wbs, wc, wcp, kd = _ssm_discretize(
        ssm_lam_re, ssm_lam_im, ssm_log_step, ssm_b_re, ssm_b_im, ssm_c_re, ssm_c_im)
    small = {
        "gm_w": gm_w_s,
        "gm_bias": jnp.repeat(jnp.swapaxes(gm_b_s, 1, 2), GM_HEAD_DIM, axis=2),
        "gm_g": gm_ln_g.reshape(depth, GM_WIDTH), "gm_b": gm_ln_b.reshape(depth, GM_WIDTH),
        "lam_r": lam_r, "lam_i": lam_i, "d": ssm_d, "glu_b": glu_b,
        "ln_g": ln_g, "ln_b": ln_b,
    }
    w_in16, w_out16 = w_in.astype(BF16), w_out.astype(BF16)
    glu_w16 = glu_w.astype(BF16)
    kt, vv = _kv_project(mem, jnp.swapaxes(xa_w_k, 1, 2).astype(BF16), xa_w_v.astype(BF16))
    for l in range(depth):
        big = {"w_in": w_in16[l], "w_out": w_out16[l], "wbs": wbs[l], "wc": wc[l],
               "wcp": wcp[l], "kd": kd[l], "glu_w": glu_w16[l]}
        x = _layer(x, l, kt, vv, big, small, alpha)
    return x
```

```python
import functools
import math

import jax
import jax.numpy as jnp
import numpy as np
from jax import lax
from jax.experimental import pallas as pl
from jax.experimental.pallas import tpu as pltpu

F32 = jnp.float32
BF16 = jnp.bfloat16

D_MODEL = 1024
GM_WIDTH = 512
GM_HEADS = 4
GM_HEAD_DIM = 128
CHUNK = 128
SSM_WIDTH = 256
SSM_GROUP = 16
SSM_GROUPS = 16
SSM_STATE = 64
N_STATE = SSM_GROUPS * SSM_STATE
XA_WIDTH = 256
XA_HEADS = 4
XA_HEAD_DIM = 64
IN_WIDTH = 3 * GM_WIDTH + 2 * SSM_WIDTH + 2 * XA_WIDTH
LN_EPS = 1e-5

C_U, C_V, C_GA = 0, GM_WIDTH, 2 * GM_WIDTH
C_XB = 3 * GM_WIDTH
C_GB = C_XB + SSM_WIDTH
C_Q = C_GB + SSM_WIDTH
C_GX = C_Q + XA_WIDTH
Y_A, Y_B, Y_X = 0, GM_WIDTH, GM_WIDTH + SSM_WIDTH

BLOCK_B = 8
BLOCK_T = CHUNK
ROWS = BLOCK_B * BLOCK_T
ROWS_PER_ITER = 2
S5_CHUNKS = 4
S5_STRIDE = 2
KD_OFFSETS = tuple(SSM_WIDTH * j * (j + 1) // 2 for j in range(S5_STRIDE - 1))
KD_ROWS = SSM_WIDTH * S5_STRIDE * (S5_STRIDE - 1) // 2
KV_BLOCK_B = 4
LANES = 128
PERM_PITCH = BLOCK_T + 8
VMEM_LIMIT_BYTES = 60 * 1024 * 1024

_LOG2E = 1.0 / math.log(2.0)
_GELU_K1 = -2.0 * math.sqrt(2.0 / math.pi) * _LOG2E
_GELU_K3 = _GELU_K1 * 0.044715


def _gelu(x):
    return x * (1.0 / (1.0 + jnp.exp2(x * (_GELU_K1 + _GELU_K3 * (x * x)))))


def _sigmoid(x):
    return 1.0 / (1.0 + jnp.exp2(x * (-_LOG2E)))


def _silu(x):
    return x * _sigmoid(x)


def _dot(a, b):
    return jnp.dot(a, b, preferred_element_type=F32)


def _aligned(v, m):
    return v if isinstance(v, int) else pl.multiple_of(v, m)


def _kv_kernel(mem_ref, wkt_ref, wv_ref, kt_ref, v_ref):
    depth = wkt_ref.shape[0]
    for b in range(KV_BLOCK_B):
        m16 = mem_ref[b].astype(BF16)
        for l in range(depth):
            kt = lax.dot_general(wkt_ref[l], m16, (((1,), (1,)), ((), ())),
                                 preferred_element_type=F32)
            kt_ref[l, b] = kt.astype(BF16)
            v_ref[l, b] = _dot(m16, wv_ref[l]).astype(BF16)


def _layer_kernel(alpha, layer,
                  x_ref, xn_ref, kt_ref, v_ref, w_in_ref, w_out_ref, wbs_ref, wc_ref, wcp_ref,
                  kd_ref, gluw_ref,
                  gmw_ref, gmbias_ref, gmg_ref, gmb_ref, lamr_ref, lami_ref, d_ref,
                  glub_ref, lng_ref, lnb_ref,
                  o_ref,
                  z0_ref, z1_ref, wm_ref, y_ref, perm_ref, xbp_ref, hb_ref, gateb_ref):
    j = pl.program_id(1)
    lrow = slice(layer, layer + 1)
    z_refs = (z0_ref, z1_ref)
    assert len(z_refs) == ROWS_PER_ITER

    @pl.when(j == 0)
    def _():
        hb_ref[0:BLOCK_B, :] = jnp.zeros((BLOCK_B, 2 * N_STATE), F32)

    tri = (lax.broadcasted_iota(jnp.int32, (CHUNK, CHUNK), 0)
           >= lax.broadcasted_iota(jnp.int32, (CHUNK, CHUNK), 1))
    for hp in range(GM_HEADS // 2):
        wm_ref[hp] = jnp.concatenate(
            [jnp.where(tri, gmw_ref[2 * hp + i], 0.0).astype(BF16) for i in range(2)], axis=1)
    row_head = lax.broadcasted_iota(jnp.int32, (XA_WIDTH, XA_WIDTH), 0) // XA_HEAD_DIM
    col_head = lax.broadcasted_iota(jnp.int32, (XA_WIDTH, XA_WIDTH), 1) // XA_HEAD_DIM
    out_head = lax.broadcasted_iota(jnp.int32, (BLOCK_T, XA_WIDTH), 1) // XA_HEAD_DIM

    def in_proj_fills(row, z_ref, x_row=None):
        x16 = (x_ref[row] if x_row is None else x_row).astype(BF16)

        def chunk_uv():
            z_ref[:, C_U:C_GA] = _dot(x16, w_in_ref[:, C_U:C_GA])

        def chunk_ssm():
            zz = _dot(x16, w_in_ref[:, C_GA:C_Q])
            z_ref[:, C_GA:C_XB] = zz[:, 0:GM_WIDTH]
            xs = zz[:, C_XB - C_GA:C_GB - C_GA]
            p0 = _aligned(row * PERM_PITCH, 8)
            for s in range(SSM_WIDTH // LANES):
                perm_ref[s, pl.ds(p0, BLOCK_T), :] = xs[:, s * LANES:(s + 1) * LANES]
            r0 = _aligned(row * BLOCK_T, BLOCK_T)
            gateb_ref[pl.ds(r0, BLOCK_T), :] = _silu(zz[:, C_GB - C_GA:])

        def chunk_attn():
            z_ref[:, C_Q:] = _dot(x16, w_in_ref[:, C_Q:])

        return chunk_uv, chunk_ssm, chunk_attn

    def branches(row, z_ref, fills, early_fills):
        r0 = _aligned(row * BLOCK_T, BLOCK_T)
        q16 = (z_ref[:, C_Q:C_GX] * (XA_HEAD_DIM ** -0.5 * _LOG2E)).astype(BF16)
        kt = kt_ref[row]
        vv = v_ref[row]
        scores = [_dot(q16, jnp.where(row_head == h, kt, jnp.zeros_like(kt)))
                  for h in range(XA_HEADS)]
        for f in fills[:early_fills]:
            f()
        o = None
        scale = None
        for h in range(XA_HEADS):
            s = scores[h]
            p = jnp.exp2(s - jnp.max(s, axis=-1, keepdims=True))
            inv = 1.0 / jnp.sum(p, axis=-1, keepdims=True)
            oh = _dot(p.astype(BF16), jnp.where(col_head == h, vv, jnp.zeros_like(vv)))
            o = oh if o is None else o + oh
            scale = inv if scale is None else jnp.where(out_head == h, inv, scale)
        v = _gelu(z_ref[:, C_V:C_GA])
        vn16 = []
        for h in range(GM_HEADS):
            sl = slice(h * GM_HEAD_DIM, (h + 1) * GM_HEAD_DIM)
            vh = v[:, sl]
            mu = jnp.mean(vh, axis=-1, keepdims=True)
            xc = vh - mu
            var = jnp.mean(xc * xc, axis=-1, keepdims=True)
            vn = xc * lax.rsqrt(var + LN_EPS) * gmg_ref[lrow, sl] + gmb_ref[lrow, sl]
            vn16.append(vn.astype(BF16))
        mixed = []
        zero = jnp.zeros((CHUNK, GM_HEAD_DIM), BF16)
        for hp in range(GM_HEADS // 2):
            v_a, v_b = vn16[2 * hp], vn16[2 * hp + 1]
            rhs = jnp.concatenate([jnp.concatenate([v_a, zero], axis=1),
                                   jnp.concatenate([zero, v_b], axis=1)], axis=0)
            mixed.append(_dot(wm_ref[hp], rhs))
        for f in fills[early_fills:]:
            f()
        u = _gelu(z_ref[:, C_U:C_V])
        ya = u * (jnp.concatenate(mixed, axis=1) + gmbias_ref[...]) * _silu(z_ref[:, C_GA:C_XB])
        y_ref[pl.ds(r0, BLOCK_T), Y_A:Y_B] = ya.astype(BF16)
        yx = o * scale * _silu(z_ref[:, C_GX:])
        y_ref[pl.ds(r0, BLOCK_T), Y_X:] = yx.astype(BF16)

    n_z = len(z_refs)

    @pl.when((pl.program_id(0) == 0) & (j == 0))
    def _():
        for f in in_proj_fills(0, z_refs[0]):
            f()

    def row_group(g, carry):
        k0 = ROWS_PER_ITER * g
        for i in range(ROWS_PER_ITER):
            branches(k0 + i, z_refs[i % n_z], in_proj_fills(k0 + i + 1, z_refs[(i + 1) % n_z]), 3)
        return carry

    n_looped = (BLOCK_B - 1) // ROWS_PER_ITER * ROWS_PER_ITER
    lax.fori_loop(0, n_looped // ROWS_PER_ITER, row_group, 0)
    for k in range(n_looped, BLOCK_B - 1):
        branches(k, z_refs[k % n_z], in_proj_fills(k + 1, z_refs[(k + 1) % n_z]), 2)

    n_slab = SSM_WIDTH // LANES
    sup_steps = BLOCK_T // S5_STRIDE
    sup_rows = sup_steps * BLOCK_B

    for t in range(BLOCK_T):
        tp, jj = divmod(t, S5_STRIDE)
        for s in range(n_slab):
            lane0 = jj * SSM_WIDTH + s * LANES
            xbp_ref[tp * BLOCK_B:(tp + 1) * BLOCK_B, lane0:lane0 + LANES] = (
                perm_ref[s, pl.ds(t, BLOCK_B, stride=PERM_PITCH), :])

    def g_proj(ri):
        def f():
            hb_ref[BLOCK_B:BLOCK_B + sup_rows, ri * N_STATE:(ri + 1) * N_STATE] = _dot(
                xbp_ref[...].astype(BF16), wbs_ref[ri])
        return f

    branches(BLOCK_B - 1, z_refs[(BLOCK_B - 1) % n_z], (g_proj(0), g_proj(1)), 2)

    lam_r = jnp.broadcast_to(lamr_ref[lrow, :], (BLOCK_B, N_STATE))
    lam_i = jnp.broadcast_to(lami_ref[lrow, :], (BLOCK_B, N_STATE))

    steps = sup_steps // S5_CHUNKS
    chunk = steps * BLOCK_B
    state = [hb_ref[0:BLOCK_B, 0:N_STATE], hb_ref[0:BLOCK_B, N_STATE:]]

    def scan_chunk(c):
        hr, hi = state
        for tp in range(c * steps, (c + 1) * steps):
            rows = slice((tp + 1) * BLOCK_B, (tp + 2) * BLOCK_B)
            gr = hb_ref[rows, 0:N_STATE]
            gi = hb_ref[rows, N_STATE:]
            hr, hi = lam_r * hr - lam_i * hi + gr, lam_r * hi + lam_i * hr + gi
            hb_ref[rows, 0:N_STATE] = hr
            hb_ref[rows, N_STATE:] = hi
        state[0], state[1] = hr, hi

    def c_proj(c):
        r0 = c * chunk
        x_c = xbp_ref[r0:r0 + chunk, :]
        d = d_ref[lrow, :]
        h_prev = hb_ref[r0:r0 + chunk, :].astype(BF16)
        ys = []
        for jj in range(S5_STRIDE - 1):
            k0 = KD_OFFSETS[jj]
            n_in = (jj + 1) * SSM_WIDTH
            ys.append(_dot(h_prev, wcp_ref[jj])
                      + _dot(x_c[:, 0:n_in].astype(BF16), kd_ref[k0:k0 + n_in, :])
                      + d * x_c[:, jj * SSM_WIDTH:(jj + 1) * SSM_WIDTH])
        h_last = hb_ref[BLOCK_B + r0:BLOCK_B + r0 + chunk, :].astype(BF16)
        ys.append(_dot(h_last, wc_ref[...]) + d * x_c[:, (S5_STRIDE - 1) * SSM_WIDTH:])
        return ys

    def glu(c, ys):
        r0 = c * chunk
        for jj, y in enumerate(ys):
            yg = _gelu(y)
            xbp_ref[r0:r0 + chunk, jj * SSM_WIDTH:(jj + 1) * SSM_WIDTH] = (
                yg * _sigmoid(_dot(yg.astype(BF16), gluw_ref[...]) + glub_ref[lrow, :]))

    ys_prev = None
    for c in range(S5_CHUNKS):
        scan_chunk(c)
        ys_c = c_proj(c)
        if ys_prev is not None:
            glu(c - 1, ys_prev)
        ys_prev = ys_c
    glu(S5_CHUNKS - 1, ys_prev)
    hb_ref[0:BLOCK_B, 0:N_STATE] = state[0]
    hb_ref[0:BLOCK_B, N_STATE:] = state[1]

    for t in range(BLOCK_T):
        tp, jj = divmod(t, S5_STRIDE)
        for s in range(n_slab):
            lane0 = jj * SSM_WIDTH + s * LANES
            perm_ref[s, pl.ds(t, BLOCK_B, stride=PERM_PITCH), :] = (
                xbp_ref[tp * BLOCK_B:(tp + 1) * BLOCK_B, lane0:lane0 + LANES])

    for b in range(BLOCK_B):
        ybb = jnp.concatenate(
            [perm_ref[s, b * PERM_PITCH:b * PERM_PITCH + BLOCK_T, :] for s in range(n_slab)],
            axis=1)
        rows = slice(b * BLOCK_T, (b + 1) * BLOCK_T)
        y_ref[rows, Y_B:Y_X] = (ybb * gateb_ref[rows, :]).astype(BF16)

    for b in range(BLOCK_B):
        rows = slice(b * BLOCK_T, (b + 1) * BLOCK_T)
        r = alpha * x_ref[b] + _dot(y_ref[rows, :], w_out_ref[...])
        if b == BLOCK_B - 1:
            for f in in_proj_fills(0, z_refs[0], x_row=xn_ref[0]):
                f()
        mu = jnp.mean(r, axis=-1, keepdims=True)
        rc = r - mu
        var = jnp.mean(rc * rc, axis=-1, keepdims=True)
        o_ref[b] = rc * lax.rsqrt(var + LN_EPS) * lng_ref[lrow, :] + lnb_ref[lrow, :]


def _full(shape):
    nd = len(shape)
    return pl.BlockSpec(shape, lambda i, j, _nd=nd: (0,) * _nd)


def _of_layer(shape, layer):
    nd = len(shape)
    return pl.BlockSpec((None,) + tuple(shape),
                        lambda i, j, _l=layer, _nd=nd: (_l,) + (0,) * _nd)


def _kv_project(mem, wkt16, wv16):
    bsz, m_len, d = mem.shape
    depth = wkt16.shape[0]
    kt_sds = jax.ShapeDtypeStruct((depth, bsz, XA_WIDTH, m_len), BF16)
    v_sds = jax.ShapeDtypeStruct((depth, bsz, m_len, XA_WIDTH), BF16)
    return pl.pallas_call(
        _kv_kernel,
        grid=(bsz // KV_BLOCK_B,),
        in_specs=[pl.BlockSpec((KV_BLOCK_B, m_len, d), lambda i: (i, 0, 0)),
                  pl.BlockSpec((depth, XA_WIDTH, d), lambda i: (0, 0, 0)),
                  pl.BlockSpec((depth, d, XA_WIDTH), lambda i: (0, 0, 0))],
        out_specs=[pl.BlockSpec((depth, KV_BLOCK_B, XA_WIDTH, m_len), lambda i: (0, i, 0, 0)),
                   pl.BlockSpec((depth, KV_BLOCK_B, m_len, XA_WIDTH), lambda i: (0, i, 0, 0))],
        out_shape=[kt_sds, v_sds],
        name="kv_project",
    )(mem, wkt16, wv16)


def _layer(x, layer, kt, vv, big, small, alpha):
    bsz, seq, d = x.shape
    depth, _, _, m_len = kt.shape
    grid = (bsz // BLOCK_B, seq // BLOCK_T)
    n_steps = grid[0] * grid[1]

    def next_row0(i, j):
        s = jnp.minimum(i * grid[1] + j + 1, n_steps - 1)
        return (s // grid[1]) * BLOCK_B, s % grid[1], 0

    in_specs = [
        pl.BlockSpec((BLOCK_B, BLOCK_T, d), lambda i, j: (i, j, 0)),
        pl.BlockSpec((1, BLOCK_T, d), next_row0),
        pl.BlockSpec((None, BLOCK_B, XA_WIDTH, m_len), lambda i, j: (layer, i, 0, 0)),
        pl.BlockSpec((None, BLOCK_B, m_len, XA_WIDTH), lambda i, j: (layer, i, 0, 0)),
        _full((d, IN_WIDTH)), _full((d, d)),
        _full((2, S5_STRIDE * SSM_WIDTH, N_STATE)), _full((2 * N_STATE, SSM_WIDTH)),
        _full((S5_STRIDE - 1, 2 * N_STATE, SSM_WIDTH)), _full((KD_ROWS, SSM_WIDTH)),
        _full((SSM_WIDTH, SSM_WIDTH)),
        _of_layer((GM_HEADS, CHUNK, CHUNK), layer),
        _of_layer((CHUNK, GM_WIDTH), layer),
        _full((depth, GM_WIDTH)), _full((depth, GM_WIDTH)),
        _full((depth, N_STATE)), _full((depth, N_STATE)),
        _full((depth, SSM_WIDTH)), _full((depth, SSM_WIDTH)),
        _full((depth, d)), _full((depth, d)),
    ]
    scratch = [
        pltpu.VMEM((BLOCK_T, IN_WIDTH), F32),
        pltpu.VMEM((BLOCK_T, IN_WIDTH), F32),
        pltpu.VMEM((GM_HEADS // 2, CHUNK, 2 * CHUNK), BF16),
        pltpu.VMEM((ROWS, d), BF16),
        pltpu.VMEM((SSM_WIDTH // LANES, BLOCK_B * PERM_PITCH, LANES), F32),
        pltpu.VMEM((ROWS // S5_STRIDE, S5_STRIDE * SSM_WIDTH), F32),
        pltpu.VMEM((ROWS // S5_STRIDE + BLOCK_B, 2 * N_STATE), F32),
        pltpu.VMEM((ROWS, SSM_WIDTH), F32),
    ]
    return pl.pallas_call(
        functools.partial(_layer_kernel, alpha, layer),
        grid=grid,
        in_specs=in_specs,
        out_specs=pl.BlockSpec((BLOCK_B, BLOCK_T, d), lambda i, j: (i, j, 0)),
        out_shape=jax.ShapeDtypeStruct(x.shape, x.dtype),
        scratch_shapes=scratch,
        compiler_params=pltpu.CompilerParams(
            dimension_semantics=("arbitrary", "arbitrary"),
            vmem_limit_bytes=VMEM_LIMIT_BYTES),
        name="trunk_layer",
    )(x, x, kt, vv, big["w_in"], big["w_out"], big["wbs"], big["wc"], big["wcp"], big["kd"],
      big["glu_w"],
      small["gm_w"], small["gm_bias"], small["gm_g"], small["gm_b"], small["lam_r"],
      small["lam_i"], small["d"], small["glu_b"], small["ln_g"], small["ln_b"])


def _ssm_discretize(lam_re, lam_im, log_step, b_re, b_im, c_re, c_im):
    depth = lam_re.shape[0]
    step = jnp.exp(log_step)[:, :, None]
    a, b = lam_re * step, lam_im * step

    def lam_pow(m):
        mag = jnp.exp(m * a)
        return mag * jnp.cos(m * b), mag * jnp.sin(m * b)

    def cmul(xr, xi, yr, yi):
        return xr * yr - xi * yi, xr * yi + xi * yr

    def group_block_diag(m, per_row, per_col):
        rows, cols = m.shape[-2], SSM_GROUPS * per_col
        spread = (np.arange(per_col)[:, None] == np.arange(cols)[None, :] % per_col)
        tiled = jnp.einsum('...p,pq->...q', m, spread.astype(np.float32),
                           precision=lax.Precision.HIGHEST, preferred_element_type=BF16)
        same_group = ((np.arange(rows)[:, None] // per_row) % SSM_GROUPS
                      == np.arange(cols)[None, :] // per_col)
        return jnp.where(same_group, tiled, jnp.zeros((), BF16))

    lbr, lbi = lam_pow(1)
    den = lam_re * lam_re + lam_im * lam_im
    fr = ((lbr - 1.0) * lam_re + lbi * lam_im) / den
    fi = (lbi * lam_re - (lbr - 1.0) * lam_im) / den
    bbr, bbi = cmul(fr[..., None], fi[..., None], b_re, b_im)

    coef = [cmul(*(q[..., None] for q in lam_pow(S5_STRIDE - 1 - j)), bbr, bbi)
            for j in range(S5_STRIDE)]
    w_in_side = jnp.stack([jnp.stack([c[ri] for c in coef], axis=1) for ri in range(2)],
                          axis=1)
    w_in_side = jnp.transpose(w_in_side, (0, 1, 2, 3, 5, 4)).reshape(
        depth, 2, S5_STRIDE * SSM_WIDTH, SSM_STATE)
    wbs = group_block_diag(w_in_side, SSM_GROUP, SSM_STATE)

    def c_times_lam_pow(m):
        pr, pi = lam_pow(m)
        return cmul(c_re, c_im, pr[:, :, None, :], pi[:, :, None, :])

    c_pows = [c_times_lam_pow(m) for m in range(S5_STRIDE)]
    w_out_side = jnp.stack([jnp.stack([cr, -ci], axis=1) for cr, ci in c_pows], axis=1)
    w_out_side = jnp.transpose(w_out_side, (0, 1, 2, 3, 5, 4)).reshape(
        depth, S5_STRIDE, 2, N_STATE, SSM_GROUP)
    wc_all = group_block_diag(w_out_side, SSM_STATE, SSM_GROUP).reshape(
        depth, S5_STRIDE, 2 * N_STATE, SSM_WIDTH)
    wc, wcp = wc_all[:, 0], wc_all[:, 1:]

    k_dir = jnp.stack([jnp.einsum('lgop,lgpi->lgio', cr, bbr)
                       - jnp.einsum('lgop,lgpi->lgio', ci, bbi)
                       for cr, ci in c_pows[:S5_STRIDE - 1]], axis=1)
    k_dir = group_block_diag(
        k_dir.reshape(depth, S5_STRIDE - 1, SSM_WIDTH, SSM_GROUP), SSM_GROUP, SSM_GROUP)
    kd = jnp.concatenate([k_dir[:, j - i] for j in range(S5_STRIDE - 1) for i in range(j + 1)],
                         axis=1)
    lsr, lsi = lam_pow(S5_STRIDE)
    return lsr.reshape(depth, N_STATE), lsi.reshape(depth, N_STATE), wbs, wc, wcp, kd


def kernel(x, mem, w_in, gm_w_s, gm_b_s, gm_ln_g, gm_ln_b, ssm_lam_re, ssm_lam_im,
           ssm_log_step, ssm_b_re, ssm_b_im, ssm_c_re, ssm_c_im, ssm_d, glu_w, glu_b,
           xa_w_k, xa_w_v, w_out, ln_g, ln_b):
    depth = w_in.shape[0]
    alpha = float((2 * depth) ** 0.25)
    lam_r, lam_i, wbs, wc, wcp, kd = _ssm_discretize(
        ssm_lam_re, ssm_lam_im, ssm_log_step, ssm_b_re, ssm_b_im, ssm_c_re, ssm_c_im)
    small = {
        "gm_w": gm_w_s,
        "gm_bias": jnp.repeat(jnp.swapaxes(gm_b_s, 1, 2), GM_HEAD_DIM, axis=2),
        "gm_g": gm_ln_g.reshape(depth, GM_WIDTH), "gm_b": gm_ln_b.reshape(depth, GM_WIDTH),
        "lam_r": lam_r, "lam_i": lam_i, "d": ssm_d, "glu_b": glu_b,
        "ln_g": ln_g, "ln_b": ln_b,
    }
    w_in16, w_out16 = w_in.astype(BF16), w_out.astype(BF16)
    glu_w16 = glu_w.astype(BF16)
    kt, vv = _kv_project(mem, jnp.swapaxes(xa_w_k, 1, 2).astype(BF16), xa_w_v.astype(BF16))
    for l in range(depth):
        big = {"w_in": w_in16[l], "w_out": w_out16[l], "wbs": wbs[l], "wc": wc[l],
               "wcp": wcp[l], "kd": kd[l], "glu_w": glu_w16[l]}
        x = _layer(x, l, kt, vv, big, small, alpha)
    return x
```

```python
import functools
import math

import jax
import jax.numpy as jnp
import numpy as np
from jax import lax
from jax.experimental import pallas as pl
from jax.experimental.pallas import tpu as pltpu

F32 = jnp.float32
BF16 = jnp.bfloat16

D_MODEL = 1024
GM_WIDTH = 512
GM_HEADS = 4
GM_HEAD_DIM = 128
CHUNK = 128
SSM_WIDTH = 256
SSM_GROUP = 16
SSM_GROUPS = 16
SSM_STATE = 64
N_STATE = SSM_GROUPS * SSM_STATE
XA_WIDTH = 256
XA_HEADS = 4
XA_HEAD_DIM = 64
IN_WIDTH = 3 * GM_WIDTH + 2 * SSM_WIDTH + 2 * XA_WIDTH
LN_EPS = 1e-5

C_U, C_V, C_GA = 0, GM_WIDTH, 2 * GM_WIDTH
C_XB = 3 * GM_WIDTH
C_GB = C_XB + SSM_WIDTH
C_Q = C_GB + SSM_WIDTH
C_GX = C_Q + XA_WIDTH
Y_A, Y_B, Y_X = 0, GM_WIDTH, GM_WIDTH + SSM_WIDTH

BLOCK_B = 8
BLOCK_T = CHUNK
ROWS = BLOCK_B * BLOCK_T
ROWS_PER_ITER = 2
S5_CHUNKS = 4
S5_STRIDE = 2
KD_OFFSETS = tuple(SSM_WIDTH * j * (j + 1) // 2 for j in range(S5_STRIDE - 1))
KD_ROWS = SSM_WIDTH * S5_STRIDE * (S5_STRIDE - 1) // 2
KV_BLOCK_B = 4
LANES = 128
PERM_PITCH = BLOCK_T + 8
VMEM_LIMIT_BYTES = 60 * 1024 * 1024

_LOG2E = 1.0 / math.log(2.0)
_GELU_K1 = -2.0 * math.sqrt(2.0 / math.pi) * _LOG2E
_GELU_K3 = _GELU_K1 * 0.044715


def _gelu(x):
    return x * (1.0 / (1.0 + jnp.exp2(x * (_GELU_K1 + _GELU_K3 * (x * x)))))


def _sigmoid(x):
    return 1.0 / (1.0 + jnp.exp2(x * (-_LOG2E)))


def _silu(x):
    return x * _sigmoid(x)


def _dot(a, b):
    return jnp.dot(a, b, preferred_element_type=F32)


def _aligned(v, m):
    return v if isinstance(v, int) else pl.multiple_of(v, m)


def _kv_kernel(mem_ref, wkt_ref, wv_ref, kt_ref, v_ref):
    depth = wkt_ref.shape[0]
    for b in range(KV_BLOCK_B):
        m16 = mem_ref[b].astype(BF16)
        for l in range(depth):
            kt = lax.dot_general(wkt_ref[l], m16, (((1,), (1,)), ((), ())),
                                 preferred_element_type=F32)
            kt_ref[l, b] = kt.astype(BF16)
            v_ref[l, b] = _dot(m16, wv_ref[l]).astype(BF16)


def _layer_kernel(alpha, layer,
                  x_ref, xn_ref, kt_ref, v_ref, w_in_ref, w_out_ref, wbs_ref, wc_ref, wcp_ref,
                  kd_ref, gluw_ref,
                  gmw_ref, gmbias_ref, gmg_ref, gmb_ref, lamr_ref, lami_ref, d_ref,
                  glub_ref, lng_ref, lnb_ref,
                  o_ref,
                  z0_ref, z1_ref, wm_ref, y_ref, perm_ref, xbp_ref, hb_ref, gateb_ref):
    j = pl.program_id(1)
    lrow = slice(layer, layer + 1)
    z_refs = (z0_ref, z1_ref)
    assert len(z_refs) == ROWS_PER_ITER

    @pl.when(j == 0)
    def _():
        hb_ref[0:BLOCK_B, :] = jnp.zeros((BLOCK_B, 2 * N_STATE), F32)

    tri = (lax.broadcasted_iota(jnp.int32, (CHUNK, CHUNK), 0)
           >= lax.broadcasted_iota(jnp.int32, (CHUNK, CHUNK), 1))
    for hp in range(GM_HEADS // 2):
        wm_ref[hp] = jnp.concatenate(
            [jnp.where(tri, gmw_ref[2 * hp + i], 0.0).astype(BF16) for i in range(2)], axis=1)
    row_head = lax.broadcasted_iota(jnp.int32, (XA_WIDTH, XA_WIDTH), 0) // XA_HEAD_DIM
    col_head = lax.broadcasted_iota(jnp.int32, (XA_WIDTH, XA_WIDTH), 1) // XA_HEAD_DIM
    out_head = lax.broadcasted_iota(jnp.int32, (BLOCK_T, XA_WIDTH), 1) // XA_HEAD_DIM

    def in_proj_fills(row, z_ref, x_row=None):
        x16 = (x_ref[row] if x_row is None else x_row).astype(BF16)

        def chunk_uv():
            z_ref[:, C_U:C_GA] = _dot(x16, w_in_ref[:, C_U:C_GA])

        def chunk_ssm():
            zz = _dot(x16, w_in_ref[:, C_GA:C_Q])
            z_ref[:, C_GA:C_XB] = zz[:, 0:GM_WIDTH]
            xs = zz[:, C_XB - C_GA:C_GB - C_GA]
            p0 = _aligned(row * PERM_PITCH, 8)
            for s in range(SSM_WIDTH // LANES):
                perm_ref[s, pl.ds(p0, BLOCK_T), :] = xs[:, s * LANES:(s + 1) * LANES]
            r0 = _aligned(row * BLOCK_T, BLOCK_T)
            gateb_ref[pl.ds(r0, BLOCK_T), :] = _silu(zz[:, C_GB - C_GA:])

        def chunk_attn():
            z_ref[:, C_Q:] = _dot(x16, w_in_ref[:, C_Q:])

        return chunk_uv, chunk_ssm, chunk_attn

    def branches(row, z_ref, fills, early_fills):
        r0 = _aligned(row * BLOCK_T, BLOCK_T)
        q16 = (z_ref[:, C_Q:C_GX] * (XA_HEAD_DIM ** -0.5 * _LOG2E)).astype(BF16)
        kt = kt_ref[row]
        vv = v_ref[row]
        scores = [_dot(q16, jnp.where(row_head == h, kt, jnp.zeros_like(kt)))
                  for h in range(XA_HEADS)]
        for f in fills[:early_fills]:
            f()
        o = None
        scale = None
        for h in range(XA_HEADS):
            s = scores[h]
            p = jnp.exp2(s - jnp.max(s, axis=-1, keepdims=True))
            inv = 1.0 / jnp.sum(p, axis=-1, keepdims=True)
            oh = _dot(p.astype(BF16), jnp.where(col_head == h, vv, jnp.zeros_like(vv)))
            o = oh if o is None else o + oh
            scale = inv if scale is None else jnp.where(out_head == h, inv, scale)
        v = _gelu(z_ref[:, C_V:C_GA])
        vn16 = []
        for h in range(GM_HEADS):
            sl = slice(h * GM_HEAD_DIM, (h + 1) * GM_HEAD_DIM)
            vh = v[:, sl]
            mu = jnp.mean(vh, axis=-1, keepdims=True)
            xc = vh - mu
            var = jnp.mean(xc * xc, axis=-1, keepdims=True)
            vn = xc * lax.rsqrt(var + LN_EPS) * gmg_ref[lrow, sl] + gmb_ref[lrow, sl]
            vn16.append(vn.astype(BF16))
        mixed = []
        zero = jnp.zeros((CHUNK, GM_HEAD_DIM), BF16)
        for hp in range(GM_HEADS // 2):
            v_a, v_b = vn16[2 * hp], vn16[2 * hp + 1]
            rhs = jnp.concatenate([jnp.concatenate([v_a, zero], axis=1),
                                   jnp.concatenate([zero, v_b], axis=1)], axis=0)
            mixed.append(_dot(wm_ref[hp], rhs))
        for f in fills[early_fills:]:
            f()
        u = _gelu(z_ref[:, C_U:C_V])
        ya = u * (jnp.concatenate(mixed, axis=1) + gmbias_ref[...]) * _silu(z_ref[:, C_GA:C_XB])
        y_ref[pl.ds(r0, BLOCK_T), Y_A:Y_B] = ya.astype(BF16)
        yx = o * scale * _silu(z_ref[:, C_GX:])
        y_ref[pl.ds(r0, BLOCK_T), Y_X:] = yx.astype(BF16)

    n_z = len(z_refs)

    @pl.when((pl.program_id(0) == 0) & (j == 0))
    def _():
        for f in in_proj_fills(0, z_refs[0]):
            f()

    def row_group(g, carry):
        k0 = ROWS_PER_ITER * g
        for i in range(ROWS_PER_ITER):
            branches(k0 + i, z_refs[i % n_z], in_proj_fills(k0 + i + 1, z_refs[(i + 1) % n_z]), 3)
        return carry

    n_looped = (BLOCK_B - 1) // ROWS_PER_ITER * ROWS_PER_ITER
    lax.fori_loop(0, n_looped // ROWS_PER_ITER, row_group, 0)
    for k in range(n_looped, BLOCK_B - 1):
        branches(k, z_refs[k % n_z], in_proj_fills(k + 1, z_refs[(k + 1) % n_z]), 2)

    n_slab = SSM_WIDTH // LANES
    sup_steps = BLOCK_T // S5_STRIDE
    sup_rows = sup_steps * BLOCK_B

    for t in range(BLOCK_T):
        tp, jj = divmod(t, S5_STRIDE)
        for s in range(n_slab):
            lane0 = jj * SSM_WIDTH + s * LANES
            xbp_ref[tp * BLOCK_B:(tp + 1) * BLOCK_B, lane0:lane0 + LANES] = (
                perm_ref[s, pl.ds(t, BLOCK_B, stride=PERM_PITCH), :])

    def g_proj(half, ri):
        def f():
            r0 = half * (sup_rows // 2)
            hb_ref[BLOCK_B + r0:BLOCK_B + r0 + sup_rows // 2,
                   ri * N_STATE:(ri + 1) * N_STATE] = _dot(
                       xbp_ref[r0:r0 + sup_rows // 2, :].astype(BF16), wbs_ref[ri])
        return f

    branches(BLOCK_B - 1, z_refs[(BLOCK_B - 1) % n_z],
             (g_proj(0, 0), g_proj(0, 1), g_proj(1, 0), g_proj(1, 1)), 2)

    lam_r = jnp.broadcast_to(lamr_ref[lrow, :], (BLOCK_B, N_STATE))
    lam_i = jnp.broadcast_to(lami_ref[lrow, :], (BLOCK_B, N_STATE))

    steps = sup_steps // S5_CHUNKS
    chunk = steps * BLOCK_B
    state = [hb_ref[0:BLOCK_B, 0:N_STATE], hb_ref[0:BLOCK_B, N_STATE:]]

    def scan_chunk(c):
        hr, hi = state
        for tp in range(c * steps, (c + 1) * steps):
            rows = slice((tp + 1) * BLOCK_B, (tp + 2) * BLOCK_B)
            gr = hb_ref[rows, 0:N_STATE]
            gi = hb_ref[rows, N_STATE:]
            hr, hi = lam_r * hr - lam_i * hi + gr, lam_r * hi + lam_i * hr + gi
            hb_ref[rows, 0:N_STATE] = hr
            hb_ref[rows, N_STATE:] = hi
        state[0], state[1] = hr, hi

    def c_proj(c):
        r0 = c * chunk
        x_c = xbp_ref[r0:r0 + chunk, :]
        d = d_ref[lrow, :]
        h_prev = hb_ref[r0:r0 + chunk, :].astype(BF16)
        ys = []
        for jj in range(S5_STRIDE - 1):
            k0 = KD_OFFSETS[jj]
            n_in = (jj + 1) * SSM_WIDTH
            ys.append(_dot(h_prev, wcp_ref[jj])
                      + _dot(x_c[:, 0:n_in].astype(BF16), kd_ref[k0:k0 + n_in, :])
                      + d * x_c[:, jj * SSM_WIDTH:(jj + 1) * SSM_WIDTH])
        h_last = hb_ref[BLOCK_B + r0:BLOCK_B + r0 + chunk, :].astype(BF16)
        ys.append(_dot(h_last, wc_ref[...]) + d * x_c[:, (S5_STRIDE - 1) * SSM_WIDTH:])
        return ys

    def glu(c, ys):
        r0 = c * chunk
        for jj, y in enumerate(ys):
            yg = _gelu(y)
            xbp_ref[r0:r0 + chunk, jj * SSM_WIDTH:(jj + 1) * SSM_WIDTH] = (
                yg * _sigmoid(_dot(yg.astype(BF16), gluw_ref[...]) + glub_ref[lrow, :]))

    ys_prev = None
    for c in range(S5_CHUNKS):
        scan_chunk(c)
        ys_c = c_proj(c)
        if ys_prev is not None:
            glu(c - 1, ys_prev)
        ys_prev = ys_c
    glu(S5_CHUNKS - 1, ys_prev)
    hb_ref[0:BLOCK_B, 0:N_STATE] = state[0]
    hb_ref[0:BLOCK_B, N_STATE:] = state[1]

    for t in range(BLOCK_T):
        tp, jj = divmod(t, S5_STRIDE)
        for s in range(n_slab):
            lane0 = jj * SSM_WIDTH + s * LANES
            perm_ref[s, pl.ds(t, BLOCK_B, stride=PERM_PITCH), :] = (
                xbp_ref[tp * BLOCK_B:(tp + 1) * BLOCK_B, lane0:lane0 + LANES])

    for b in range(BLOCK_B):
        ybb = jnp.concatenate(
            [perm_ref[s, b * PERM_PITCH:b * PERM_PITCH + BLOCK_T, :] for s in range(n_slab)],
            axis=1)
        rows = slice(b * BLOCK_T, (b + 1) * BLOCK_T)
        y_ref[rows, Y_B:Y_X] = (ybb * gateb_ref[rows, :]).astype(BF16)

    for b in range(BLOCK_B):
        rows = slice(b * BLOCK_T, (b + 1) * BLOCK_T)
        r = alpha * x_ref[b] + _dot(y_ref[rows, :], w_out_ref[...])
        if b == BLOCK_B - 1:
            for f in in_proj_fills(0, z_refs[0], x_row=xn_ref[0]):
                f()
        mu = jnp.mean(r, axis=-1, keepdims=True)
        rc = r - mu
        var = jnp.mean(rc * rc, axis=-1, keepdims=True)
        o_ref[b] = rc * lax.rsqrt(var + LN_EPS) * lng_ref[lrow, :] + lnb_ref[lrow, :]


def _full(shape):
    nd = len(shape)
    return pl.BlockSpec(shape, lambda i, j, _nd=nd: (0,) * _nd)


def _of_layer(shape, layer):
    nd = len(shape)
    return pl.BlockSpec((None,) + tuple(shape),
                        lambda i, j, _l=layer, _nd=nd: (_l,) + (0,) * _nd)


def _kv_project(mem, wkt16, wv16):
    bsz, m_len, d = mem.shape
    depth = wkt16.shape[0]
    kt_sds = jax.ShapeDtypeStruct((depth, bsz, XA_WIDTH, m_len), BF16)
    v_sds = jax.ShapeDtypeStruct((depth, bsz, m_len, XA_WIDTH), BF16)
    return pl.pallas_call(
        _kv_kernel,
        grid=(bsz // KV_BLOCK_B,),
        in_specs=[pl.BlockSpec((KV_BLOCK_B, m_len, d), lambda i: (i, 0, 0)),
                  pl.BlockSpec((depth, XA_WIDTH, d), lambda i: (0, 0, 0)),
                  pl.BlockSpec((depth, d, XA_WIDTH), lambda i: (0, 0, 0))],
        out_specs=[pl.BlockSpec((depth, KV_BLOCK_B, XA_WIDTH, m_len), lambda i: (0, i, 0, 0)),
                   pl.BlockSpec((depth, KV_BLOCK_B, m_len, XA_WIDTH), lambda i: (0, i, 0, 0))],
        out_shape=[kt_sds, v_sds],
        name="kv_project",
    )(mem, wkt16, wv16)


def _layer(x, layer, kt, vv, big, small, alpha):
    bsz, seq, d = x.shape
    depth, _, _, m_len = kt.shape
    grid = (bsz // BLOCK_B, seq // BLOCK_T)
    n_steps = grid[0] * grid[1]

    def next_row0(i, j):
        s = jnp.minimum(i * grid[1] + j + 1, n_steps - 1)
        return (s // grid[1]) * BLOCK_B, s % grid[1], 0

    in_specs = [
        pl.BlockSpec((BLOCK_B, BLOCK_T, d), lambda i, j: (i, j, 0)),
        pl.BlockSpec((1, BLOCK_T, d), next_row0),
        pl.BlockSpec((None, BLOCK_B, XA_WIDTH, m_len), lambda i, j: (layer, i, 0, 0)),
        pl.BlockSpec((None, BLOCK_B, m_len, XA_WIDTH), lambda i, j: (layer, i, 0, 0)),
        _full((d, IN_WIDTH)), _full((d, d)),
        _full((2, S5_STRIDE * SSM_WIDTH, N_STATE)), _full((2 * N_STATE, SSM_WIDTH)),
        _full((S5_STRIDE - 1, 2 * N_STATE, SSM_WIDTH)), _full((KD_ROWS, SSM_WIDTH)),
        _full((SSM_WIDTH, SSM_WIDTH)),
        _of_layer((GM_HEADS, CHUNK, CHUNK), layer),
        _of_layer((CHUNK, GM_WIDTH), layer),
        _full((depth, GM_WIDTH)), _full((depth, GM_WIDTH)),
        _full((depth, N_STATE)), _full((depth, N_STATE)),
        _full((depth, SSM_WIDTH)), _full((depth, SSM_WIDTH)),
        _full((depth, d)), _full((depth, d)),
    ]
    scratch = [
        pltpu.VMEM((BLOCK_T, IN_WIDTH), F32),
        pltpu.VMEM((BLOCK_T, IN_WIDTH), F32),
        pltpu.VMEM((GM_HEADS // 2, CHUNK, 2 * CHUNK), BF16),
        pltpu.VMEM((ROWS, d), BF16),
        pltpu.VMEM((SSM_WIDTH // LANES, BLOCK_B * PERM_PITCH, LANES), F32),
        pltpu.VMEM((ROWS // S5_STRIDE, S5_STRIDE * SSM_WIDTH), F32),
        pltpu.VMEM((ROWS // S5_STRIDE + BLOCK_B, 2 * N_STATE), F32),
        pltpu.VMEM((ROWS, SSM_WIDTH), F32),
    ]
    return pl.pallas_call(
        functools.partial(_layer_kernel, alpha, layer),
        grid=grid,
        in_specs=in_specs,
        out_specs=pl.BlockSpec((BLOCK_B, BLOCK_T, d), lambda i, j: (i, j, 0)),
        out_shape=jax.ShapeDtypeStruct(x.shape, x.dtype),
        scratch_shapes=scratch,
        compiler_params=pltpu.CompilerParams(
            dimension_semantics=("arbitrary", "arbitrary"),
            vmem_limit_bytes=VMEM_LIMIT_BYTES),
        name="trunk_layer",
    )(x, x, kt, vv, big["w_in"], big["w_out"], big["wbs"], big["wc"], big["wcp"], big["kd"],
      big["glu_w"],
      small["gm_w"], small["gm_bias"], small["gm_g"], small["gm_b"], small["lam_r"],
      small["lam_i"], small["d"], small["glu_b"], small["ln_g"], small["ln_b"])


def _ssm_discretize(lam_re, lam_im, log_step, b_re, b_im, c_re, c_im):
    depth = lam_re.shape[0]
    step = jnp.exp(log_step)[:, :, None]
    a, b = lam_re * step, lam_im * step

    def lam_pow(m):
        mag = jnp.exp(m * a)
        return mag * jnp.cos(m * b), mag * jnp.sin(m * b)

    def cmul(xr, xi, yr, yi):
        return xr * yr - xi * yi, xr * yi + xi * yr

    def group_block_diag(m, per_row, per_col):
        rows, cols = m.shape[-2], SSM_GROUPS * per_col
        spread = (np.arange(per_col)[:, None] == np.arange(cols)[None, :] % per_col)
        tiled = jnp.einsum('...p,pq->...q', m, spread.astype(np.float32),
                           precision=lax.Precision.HIGHEST, preferred_element_type=F32)
        same_group = ((np.arange(rows)[:, None] // per_row) % SSM_GROUPS
                      == np.arange(cols)[None, :] // per_col)
        return jnp.where(same_group, tiled.astype(BF16), jnp.zeros((), BF16))

    lbr, lbi = lam_pow(1)
    den = lam_re * lam_re + lam_im * lam_im
    fr = ((lbr - 1.0) * lam_re + lbi * lam_im) / den
    fi = (lbi * lam_re - (lbr - 1.0) * lam_im) / den
    bbr, bbi = cmul(fr[..., None], fi[..., None], b_re, b_im)

    coef = [cmul(*(q[..., None] for q in lam_pow(S5_STRIDE - 1 - j)), bbr, bbi)
            for j in range(S5_STRIDE)]
    w_in_side = jnp.stack([jnp.stack([c[ri] for c in coef], axis=1) for ri in range(2)],
                          axis=1)
    w_in_side = jnp.transpose(w_in_side, (0, 1, 2, 3, 5, 4)).reshape(
        depth, 2, S5_STRIDE * SSM_WIDTH, SSM_STATE)
    wbs = group_block_diag(w_in_side, SSM_GROUP, SSM_STATE)

    def c_times_lam_pow(m):
        pr, pi = lam_pow(m)
        return cmul(c_re, c_im, pr[:, :, None, :], pi[:, :, None, :])

    c_pows = [c_times_lam_pow(m) for m in range(S5_STRIDE)]
    w_out_side = jnp.stack([jnp.stack([cr, -ci], axis=1) for cr, ci in c_pows], axis=1)
    w_out_side = jnp.transpose(w_out_side, (0, 1, 2, 3, 5, 4)).reshape(
        depth, S5_STRIDE, 2, N_STATE, SSM_GROUP)
    wc_all = group_block_diag(w_out_side, SSM_STATE, SSM_GROUP).reshape(
        depth, S5_STRIDE, 2 * N_STATE, SSM_WIDTH)
    wc, wcp = wc_all[:, 0], wc_all[:, 1:]

    k_dir = jnp.stack([jnp.einsum('lgop,lgpi->lgio', cr, bbr)
                       - jnp.einsum('lgop,lgpi->lgio', ci, bbi)
                       for cr, ci in c_pows[:S5_STRIDE - 1]], axis=1)
    k_dir = group_block_diag(
        k_dir.reshape(depth, S5_STRIDE - 1, SSM_WIDTH, SSM_GROUP), SSM_GROUP, SSM_GROUP)
    kd = jnp.concatenate([k_dir[:, j - i] for j in range(S5_STRIDE - 1) for i in range(j + 1)],
                         axis=1)
    lsr, lsi = lam_pow(S5_STRIDE)
    return lsr.reshape(depth, N_STATE), lsi.reshape(depth, N_STATE), wbs, wc, wcp, kd


def kernel(x, mem, w_in, gm_w_s, gm_b_s, gm_ln_g, gm_ln_b, ssm_lam_re, ssm_lam_im,
           ssm_log_step, ssm_b_re, ssm_b_im, ssm_c_re, ssm_c_im, ssm_d, glu_w, glu_b,
           xa_w_k, xa_w_v, w_out, ln_g, ln_b):
    depth = w_in.shape[0]
    alpha = float((2 * depth) ** 0.25)
    lam_r, lam_i, wbs, wc, wcp, kd = _ssm_discretize(
        ssm_lam_re, ssm_lam_im, ssm_log_step, ssm_b_re, ssm_b_im, ssm_c_re, ssm_c_im)
    small = {
        "gm_w": gm_w_s,
        "gm_bias": jnp.repeat(jnp.swapaxes(gm_b_s, 1, 2), GM_HEAD_DIM, axis=2),
        "gm_g": gm_ln_g.reshape(depth, GM_WIDTH), "gm_b": gm_ln_b.reshape(depth, GM_WIDTH),
        "lam_r": lam_r, "lam_i": lam_i, "d": ssm_d, "glu_b": glu_b,
        "ln_g": ln_g, "ln_b": ln_b,
    }
    w_in16, w_out16 = w_in.astype(BF16), w_out.astype(BF16)
    glu_w16 = glu_w.astype(BF16)
    kt, vv = _kv_project(mem, jnp.swapaxes(xa_w_k, 1, 2).astype(BF16), xa_w_v.astype(BF16))
    for l in range(depth):
        big = {"w_in": w_in16[l], "w_out": w_out16[l], "wbs": wbs[l], "wc": wc[l],
               "wcp": wcp[l], "kd": kd[l], "glu_w": glu_w16[l]}
        x = _layer(x, l, kt, vv, big, small, alpha)
    return x
```

```python
import functools
import math

import jax
import jax.numpy as jnp
import numpy as np
from jax import lax
from jax.experimental import pallas as pl
from jax.experimental.pallas import tpu as pltpu

F32 = jnp.float32
BF16 = jnp.bfloat16

D_MODEL = 1024
GM_WIDTH = 512
GM_HEADS = 4
GM_HEAD_DIM = 128
CHUNK = 128
SSM_WIDTH = 256
SSM_GROUP = 16
SSM_GROUPS = 16
SSM_STATE = 64
N_STATE = SSM_GROUPS * SSM_STATE
XA_WIDTH = 256
XA_HEADS = 4
XA_HEAD_DIM = 64
IN_WIDTH = 3 * GM_WIDTH + 2 * SSM_WIDTH + 2 * XA_WIDTH
LN_EPS = 1e-5

C_U, C_V, C_GA = 0, GM_WIDTH, 2 * GM_WIDTH
C_XB = 3 * GM_WIDTH
C_GB = C_XB + SSM_WIDTH
C_Q = C_GB + SSM_WIDTH
C_GX = C_Q + XA_WIDTH
Y_A, Y_B, Y_X = 0, GM_WIDTH, GM_WIDTH + SSM_WIDTH

BLOCK_B = 8
BLOCK_T = CHUNK
ROWS = BLOCK_B * BLOCK_T
ROWS_PER_ITER = 2
S5_CHUNKS = 2
S5_STRIDE = 4
KD_OFFSETS = tuple(SSM_WIDTH * j * (j + 1) // 2 for j in range(S5_STRIDE - 1))
KD_ROWS = SSM_WIDTH * S5_STRIDE * (S5_STRIDE - 1) // 2
KV_BLOCK_B = 4
LANES = 128
PERM_PITCH = BLOCK_T + 8
VMEM_LIMIT_BYTES = 60 * 1024 * 1024

_LOG2E = 1.0 / math.log(2.0)
_GELU_K1 = -2.0 * math.sqrt(2.0 / math.pi) * _LOG2E
_GELU_K3 = _GELU_K1 * 0.044715


def _gelu(x):
    return x * (1.0 / (1.0 + jnp.exp2(x * (_GELU_K1 + _GELU_K3 * (x * x)))))


def _sigmoid(x):
    return 1.0 / (1.0 + jnp.exp2(x * (-_LOG2E)))


def _silu(x):
    return x * _sigmoid(x)


def _dot(a, b):
    return jnp.dot(a, b, preferred_element_type=F32)


def _aligned(v, m):
    return v if isinstance(v, int) else pl.multiple_of(v, m)


def _kv_kernel(mem_ref, wkt_ref, wv_ref, kt_ref, v_ref):
    depth = wkt_ref.shape[0]
    for b in range(KV_BLOCK_B):
        m16 = mem_ref[b].astype(BF16)
        for l in range(depth):
            kt = lax.dot_general(wkt_ref[l], m16, (((1,), (1,)), ((), ())),
                                 preferred_element_type=F32)
            kt_ref[l, b] = kt.astype(BF16)
            v_ref[l, b] = _dot(m16, wv_ref[l]).astype(BF16)


def _layer_kernel(alpha, layer,
                  x_ref, xn_ref, kt_ref, v_ref, w_in_ref, w_out_ref, wbs_ref, wc_ref, wcp_ref,
                  kd_ref, gluw_ref,
                  gmw_ref, gmbias_ref, gmg_ref, gmb_ref, lamr_ref, lami_ref, d_ref,
                  glub_ref, lng_ref, lnb_ref,
                  o_ref,
                  z0_ref, z1_ref, wm_ref, y_ref, perm_ref, xbp_ref, hb_ref, gateb_ref):
    j = pl.program_id(1)
    lrow = slice(layer, layer + 1)
    z_refs = (z0_ref, z1_ref)
    assert len(z_refs) == ROWS_PER_ITER

    @pl.when(j == 0)
    def _():
        hb_ref[0:BLOCK_B, :] = jnp.zeros((BLOCK_B, 2 * N_STATE), F32)

    tri = (lax.broadcasted_iota(jnp.int32, (CHUNK, CHUNK), 0)
           >= lax.broadcasted_iota(jnp.int32, (CHUNK, CHUNK), 1))
    for hp in range(GM_HEADS // 2):
        wm_ref[hp] = jnp.concatenate(
            [jnp.where(tri, gmw_ref[2 * hp + i], 0.0).astype(BF16) for i in range(2)], axis=1)
    row_head = lax.broadcasted_iota(jnp.int32, (XA_WIDTH, XA_WIDTH), 0) // XA_HEAD_DIM
    col_head = lax.broadcasted_iota(jnp.int32, (XA_WIDTH, XA_WIDTH), 1) // XA_HEAD_DIM
    out_head = lax.broadcasted_iota(jnp.int32, (BLOCK_T, XA_WIDTH), 1) // XA_HEAD_DIM

    def in_proj_fills(row, z_ref, x_row=None):
        x16 = (x_ref[row] if x_row is None else x_row).astype(BF16)

        def chunk_uv():
            z_ref[:, C_U:C_GA] = _dot(x16, w_in_ref[:, C_U:C_GA])

        def chunk_ssm():
            zz = _dot(x16, w_in_ref[:, C_GA:C_Q])
            z_ref[:, C_GA:C_XB] = zz[:, 0:GM_WIDTH]
            xs = zz[:, C_XB - C_GA:C_GB - C_GA]
            p0 = _aligned(row * PERM_PITCH, 8)
            for s in range(SSM_WIDTH // LANES):
                perm_ref[s, pl.ds(p0, BLOCK_T), :] = xs[:, s * LANES:(s + 1) * LANES]
            r0 = _aligned(row * BLOCK_T, BLOCK_T)
            gateb_ref[pl.ds(r0, BLOCK_T), :] = _silu(zz[:, C_GB - C_GA:])

        def chunk_attn():
            z_ref[:, C_Q:] = _dot(x16, w_in_ref[:, C_Q:])

        return chunk_uv, chunk_ssm, chunk_attn

    def branches(row, z_ref, fills, early_fills):
        r0 = _aligned(row * BLOCK_T, BLOCK_T)
        q16 = (z_ref[:, C_Q:C_GX] * (XA_HEAD_DIM ** -0.5 * _LOG2E)).astype(BF16)
        kt = kt_ref[row]
        vv = v_ref[row]
        scores = [_dot(q16, jnp.where(row_head == h, kt, jnp.zeros_like(kt)))
                  for h in range(XA_HEADS)]
        for f in fills[:early_fills]:
            f()
        o = None
        scale = None
        for h in range(XA_HEADS):
            s = scores[h]
            p = jnp.exp2(s - jnp.max(s, axis=-1, keepdims=True))
            inv = 1.0 / jnp.sum(p, axis=-1, keepdims=True)
            oh = _dot(p.astype(BF16), jnp.where(col_head == h, vv, jnp.zeros_like(vv)))
            o = oh if o is None else o + oh
            scale = inv if scale is None else jnp.where(out_head == h, inv, scale)
        v = _gelu(z_ref[:, C_V:C_GA])
        vn16 = []
        for h in range(GM_HEADS):
            sl = slice(h * GM_HEAD_DIM, (h + 1) * GM_HEAD_DIM)
            vh = v[:, sl]
            mu = jnp.mean(vh, axis=-1, keepdims=True)
            xc = vh - mu
            var = jnp.mean(xc * xc, axis=-1, keepdims=True)
            vn = xc * lax.rsqrt(var + LN_EPS) * gmg_ref[lrow, sl] + gmb_ref[lrow, sl]
            vn16.append(vn.astype(BF16))
        mixed = []
        zero = jnp.zeros((CHUNK, GM_HEAD_DIM), BF16)
        for hp in range(GM_HEADS // 2):
            v_a, v_b = vn16[2 * hp], vn16[2 * hp + 1]
            rhs = jnp.concatenate([jnp.concatenate([v_a, zero], axis=1),
                                   jnp.concatenate([zero, v_b], axis=1)], axis=0)
            mixed.append(_dot(wm_ref[hp], rhs))
        for f in fills[early_fills:]:
            f()
        u = _gelu(z_ref[:, C_U:C_V])
        ya = u * (jnp.concatenate(mixed, axis=1) + gmbias_ref[...]) * _silu(z_ref[:, C_GA:C_XB])
        y_ref[pl.ds(r0, BLOCK_T), Y_A:Y_B] = ya.astype(BF16)
        yx = o * scale * _silu(z_ref[:, C_GX:])
        y_ref[pl.ds(r0, BLOCK_T), Y_X:] = yx.astype(BF16)

    n_z = len(z_refs)

    @pl.when((pl.program_id(0) == 0) & (j == 0))
    def _():
        for f in in_proj_fills(0, z_refs[0]):
            f()

    def row_group(g, carry):
        k0 = ROWS_PER_ITER * g
        for i in range(ROWS_PER_ITER):
            branches(k0 + i, z_refs[i % n_z], in_proj_fills(k0 + i + 1, z_refs[(i + 1) % n_z]), 3)
        return carry

    n_looped = (BLOCK_B - 1) // ROWS_PER_ITER * ROWS_PER_ITER
    lax.fori_loop(0, n_looped // ROWS_PER_ITER, row_group, 0)
    for k in range(n_looped, BLOCK_B - 1):
        branches(k, z_refs[k % n_z], in_proj_fills(k + 1, z_refs[(k + 1) % n_z]), 2)

    n_slab = SSM_WIDTH // LANES
    sup_steps = BLOCK_T // S5_STRIDE
    sup_rows = sup_steps * BLOCK_B

    for t in range(BLOCK_T):
        tp, jj = divmod(t, S5_STRIDE)
        for s in range(n_slab):
            lane0 = jj * SSM_WIDTH + s * LANES
            xbp_ref[tp * BLOCK_B:(tp + 1) * BLOCK_B, lane0:lane0 + LANES] = (
                perm_ref[s, pl.ds(t, BLOCK_B, stride=PERM_PITCH), :])

    def g_proj(ri):
        def f():
            hb_ref[BLOCK_B:BLOCK_B + sup_rows, ri * N_STATE:(ri + 1) * N_STATE] = _dot(
                xbp_ref[...].astype(BF16), wbs_ref[ri])
        return f

    branches(BLOCK_B - 1, z_refs[(BLOCK_B - 1) % n_z], (g_proj(0), g_proj(1)), 2)

    lam_r = jnp.broadcast_to(lamr_ref[lrow, :], (BLOCK_B, N_STATE))
    lam_i = jnp.broadcast_to(lami_ref[lrow, :], (BLOCK_B, N_STATE))

    steps = sup_steps // S5_CHUNKS
    chunk = steps * BLOCK_B
    state = [hb_ref[0:BLOCK_B, 0:N_STATE], hb_ref[0:BLOCK_B, N_STATE:]]

    def scan_chunk(c):
        hr, hi = state
        for tp in range(c * steps, (c + 1) * steps):
            rows = slice((tp + 1) * BLOCK_B, (tp + 2) * BLOCK_B)
            gr = hb_ref[rows, 0:N_STATE]
            gi = hb_ref[rows, N_STATE:]
            hr, hi = lam_r * hr - lam_i * hi + gr, lam_r * hi + lam_i * hr + gi
            hb_ref[rows, 0:N_STATE] = hr
            hb_ref[rows, N_STATE:] = hi
        state[0], state[1] = hr, hi

    def c_proj(c):
        r0 = c * chunk
        x_c = xbp_ref[r0:r0 + chunk, :]
        d = d_ref[lrow, :]
        h_prev = hb_ref[r0:r0 + chunk, :].astype(BF16)
        ys = []
        for jj in range(S5_STRIDE - 1):
            k0 = KD_OFFSETS[jj]
            n_in = (jj + 1) * SSM_WIDTH
            ys.append(_dot(h_prev, wcp_ref[jj])
                      + _dot(x_c[:, 0:n_in].astype(BF16), kd_ref[k0:k0 + n_in, :])
                      + d * x_c[:, jj * SSM_WIDTH:(jj + 1) * SSM_WIDTH])
        h_last = hb_ref[BLOCK_B + r0:BLOCK_B + r0 + chunk, :].astype(BF16)
        ys.append(_dot(h_last, wc_ref[...]) + d * x_c[:, (S5_STRIDE - 1) * SSM_WIDTH:])
        return ys

    def glu(c, ys):
        r0 = c * chunk
        for jj, y in enumerate(ys):
            yg = _gelu(y)
            xbp_ref[r0:r0 + chunk, jj * SSM_WIDTH:(jj + 1) * SSM_WIDTH] = (
                yg * _sigmoid(_dot(yg.astype(BF16), gluw_ref[...]) + glub_ref[lrow, :]))

    ys_prev = None
    for c in range(S5_CHUNKS):
        scan_chunk(c)
        ys_c = c_proj(c)
        if ys_prev is not None:
            glu(c - 1, ys_prev)
        ys_prev = ys_c
    glu(S5_CHUNKS - 1, ys_prev)
    hb_ref[0:BLOCK_B, 0:N_STATE] = state[0]
    hb_ref[0:BLOCK_B, N_STATE:] = state[1]

    for t in range(BLOCK_T):
        tp, jj = divmod(t, S5_STRIDE)
        for s in range(n_slab):
            lane0 = jj * SSM_WIDTH + s * LANES
            perm_ref[s, pl.ds(t, BLOCK_B, stride=PERM_PITCH), :] = (
                xbp_ref[tp * BLOCK_B:(tp + 1) * BLOCK_B, lane0:lane0 + LANES])

    for b in range(BLOCK_B):
        ybb = jnp.concatenate(
            [perm_ref[s, b * PERM_PITCH:b * PERM_PITCH + BLOCK_T, :] for s in range(n_slab)],
            axis=1)
        rows = slice(b * BLOCK_T, (b + 1) * BLOCK_T)
        y_ref[rows, Y_B:Y_X] = (ybb * gateb_ref[rows, :]).astype(BF16)

    for b in range(BLOCK_B):
        rows = slice(b * BLOCK_T, (b + 1) * BLOCK_T)
        r = alpha * x_ref[b] + _dot(y_ref[rows, :], w_out_ref[...])
        if b == BLOCK_B - 1:
            for f in in_proj_fills(0, z_refs[0], x_row=xn_ref[0]):
                f()
        mu = jnp.mean(r, axis=-1, keepdims=True)
        rc = r - mu
        var = jnp.mean(rc * rc, axis=-1, keepdims=True)
        o_ref[b] = rc * lax.rsqrt(var + LN_EPS) * lng_ref[lrow, :] + lnb_ref[lrow, :]


def _full(shape):
    nd = len(shape)
    return pl.BlockSpec(shape, lambda i, j, _nd=nd: (0,) * _nd)


def _of_layer(shape, layer):
    nd = len(shape)
    return pl.BlockSpec((None,) + tuple(shape),
                        lambda i, j, _l=layer, _nd=nd: (_l,) + (0,) * _nd)


def _kv_project(mem, wkt16, wv16):
    bsz, m_len, d = mem.shape
    depth = wkt16.shape[0]
    kt_sds = jax.ShapeDtypeStruct((depth, bsz, XA_WIDTH, m_len), BF16)
    v_sds = jax.ShapeDtypeStruct((depth, bsz, m_len, XA_WIDTH), BF16)
    return pl.pallas_call(
        _kv_kernel,
        grid=(bsz // KV_BLOCK_B,),
        in_specs=[pl.BlockSpec((KV_BLOCK_B, m_len, d), lambda i: (i, 0, 0)),
                  pl.BlockSpec((depth, XA_WIDTH, d), lambda i: (0, 0, 0)),
                  pl.BlockSpec((depth, d, XA_WIDTH), lambda i: (0, 0, 0))],
        out_specs=[pl.BlockSpec((depth, KV_BLOCK_B, XA_WIDTH, m_len), lambda i: (0, i, 0, 0)),
                   pl.BlockSpec((depth, KV_BLOCK_B, m_len, XA_WIDTH), lambda i: (0, i, 0, 0))],
        out_shape=[kt_sds, v_sds],
        name="kv_project",
    )(mem, wkt16, wv16)


def _layer(x, layer, kt, vv, big, small, alpha):
    bsz, seq, d = x.shape
    depth, _, _, m_len = kt.shape
    grid = (bsz // BLOCK_B, seq // BLOCK_T)
    n_steps = grid[0] * grid[1]

    def next_row0(i, j):
        s = jnp.minimum(i * grid[1] + j + 1, n_steps - 1)
        return (s // grid[1]) * BLOCK_B, s % grid[1], 0

    in_specs = [
        pl.BlockSpec((BLOCK_B, BLOCK_T, d), lambda i, j: (i, j, 0)),
        pl.BlockSpec((1, BLOCK_T, d), next_row0),
        pl.BlockSpec((None, BLOCK_B, XA_WIDTH, m_len), lambda i, j: (layer, i, 0, 0)),
        pl.BlockSpec((None, BLOCK_B, m_len, XA_WIDTH), lambda i, j: (layer, i, 0, 0)),
        _full((d, IN_WIDTH)), _full((d, d)),
        _full((2, S5_STRIDE * SSM_WIDTH, N_STATE)), _full((2 * N_STATE, SSM_WIDTH)),
        _full((S5_STRIDE - 1, 2 * N_STATE, SSM_WIDTH)), _full((KD_ROWS, SSM_WIDTH)),
        _full((SSM_WIDTH, SSM_WIDTH)),
        _of_layer((GM_HEADS, CHUNK, CHUNK), layer),
        _of_layer((CHUNK, GM_WIDTH), layer),
        _full((depth, GM_WIDTH)), _full((depth, GM_WIDTH)),
        _full((depth, N_STATE)), _full((depth, N_STATE)),
        _full((depth, SSM_WIDTH)), _full((depth, SSM_WIDTH)),
        _full((depth, d)), _full((depth, d)),
    ]
    scratch = [
        pltpu.VMEM((BLOCK_T, IN_WIDTH), F32),
        pltpu.VMEM((BLOCK_T, IN_WIDTH), F32),
        pltpu.VMEM((GM_HEADS // 2, CHUNK, 2 * CHUNK), BF16),
        pltpu.VMEM((ROWS, d), BF16),
        pltpu.VMEM((SSM_WIDTH // LANES, BLOCK_B * PERM_PITCH, LANES), F32),
        pltpu.VMEM((ROWS // S5_STRIDE, S5_STRIDE * SSM_WIDTH), F32),
        pltpu.VMEM((ROWS // S5_STRIDE + BLOCK_B, 2 * N_STATE), F32),
        pltpu.VMEM((ROWS, SSM_WIDTH), F32),
    ]
    return pl.pallas_call(
        functools.partial(_layer_kernel, alpha, layer),
        grid=grid,
        in_specs=in_specs,
        out_specs=pl.BlockSpec((BLOCK_B, BLOCK_T, d), lambda i, j: (i, j, 0)),
        out_shape=jax.ShapeDtypeStruct(x.shape, x.dtype),
        scratch_shapes=scratch,
        compiler_params=pltpu.CompilerParams(
            dimension_semantics=("arbitrary", "arbitrary"),
            vmem_limit_bytes=VMEM_LIMIT_BYTES),
        name="trunk_layer",
    )(x, x, kt, vv, big["w_in"], big["w_out"], big["wbs"], big["wc"], big["wcp"], big["kd"],
      big["glu_w"],
      small["gm_w"], small["gm_bias"], small["gm_g"], small["gm_b"], small["lam_r"],
      small["lam_i"], small["d"], small["glu_b"], small["ln_g"], small["ln_b"])


def _ssm_discretize(lam_re, lam_im, log_step, b_re, b_im, c_re, c_im):
    depth = lam_re.shape[0]
    step = jnp.exp(log_step)[:, :, None]
    a, b = lam_re * step, lam_im * step

    def lam_pow(m):
        mag = jnp.exp(m * a)
        return mag * jnp.cos(m * b), mag * jnp.sin(m * b)

    def cmul(xr, xi, yr, yi):
        return xr * yr - xi * yi, xr * yi + xi * yr

    def group_block_diag(m, per_row, per_col):
        rows, cols = m.shape[-2], SSM_GROUPS * per_col
        spread = (np.arange(per_col)[:, None] == np.arange(cols)[None, :] % per_col)
        tiled = jnp.einsum('...p,pq->...q', m, spread.astype(np.float32),
                           precision=lax.Precision.HIGHEST, preferred_element_type=F32)
        same_group = ((np.arange(rows)[:, None] // per_row) % SSM_GROUPS
                      == np.arange(cols)[None, :] // per_col)
        return jnp.where(same_group, tiled.astype(BF16), jnp.zeros((), BF16))

    lbr, lbi = lam_pow(1)
    den = lam_re * lam_re + lam_im * lam_im
    fr = ((lbr - 1.0) * lam_re + lbi * lam_im) / den
    fi = (lbi * lam_re - (lbr - 1.0) * lam_im) / den
    bbr, bbi = cmul(fr[..., None], fi[..., None], b_re, b_im)

    coef = [cmul(*(q[..., None] for q in lam_pow(S5_STRIDE - 1 - j)), bbr, bbi)
            for j in range(S5_STRIDE)]
    w_in_side = jnp.stack([jnp.stack([c[ri] for c in coef], axis=1) for ri in range(2)],
                          axis=1)
    w_in_side = jnp.transpose(w_in_side, (0, 1, 2, 3, 5, 4)).reshape(
        depth, 2, S5_STRIDE * SSM_WIDTH, SSM_STATE)
    wbs = group_block_diag(w_in_side, SSM_GROUP, SSM_STATE)

    def c_times_lam_pow(m):
        pr, pi = lam_pow(m)
        return cmul(c_re, c_im, pr[:, :, None, :], pi[:, :, None, :])

    c_pows = [c_times_lam_pow(m) for m in range(S5_STRIDE)]
    w_out_side = jnp.stack([jnp.stack([cr, -ci], axis=1) for cr, ci in c_pows], axis=1)
    w_out_side = jnp.transpose(w_out_side, (0, 1, 2, 3, 5, 4)).reshape(
        depth, S5_STRIDE, 2, N_STATE, SSM_GROUP)
    wc_all = group_block_diag(w_out_side, SSM_STATE, SSM_GROUP).reshape(
        depth, S5_STRIDE, 2 * N_STATE, SSM_WIDTH)
    wc, wcp = wc_all[:, 0], wc_all[:, 1:]

    k_dir = jnp.stack([jnp.einsum('lgop,lgpi->lgio', cr, bbr)
                       - jnp.einsum('lgop,lgpi->lgio', ci, bbi)
                       for cr, ci in c_pows[:S5_STRIDE - 1]], axis=1)
    k_dir = group_block_diag(
        k_dir.reshape(depth, S5_STRIDE - 1, SSM_WIDTH, SSM_GROUP), SSM_GROUP, SSM_GROUP)
    kd = jnp.concatenate([k_dir[:, j - i] for j in range(S5_STRIDE - 1) for i in range(j + 1)],
                         axis=1)
    lsr, lsi = lam_pow(S5_STRIDE)
    return lsr.reshape(depth, N_STATE), lsi.reshape(depth, N_STATE), wbs, wc, wcp, kd


def kernel(x, mem, w_in, gm_w_s, gm_b_s, gm_ln_g, gm_ln_b, ssm_lam_re, ssm_lam_im,
           ssm_log_step, ssm_b_re, ssm_b_im, ssm_c_re, ssm_c_im, ssm_d, glu_w, glu_b,
           xa_w_k, xa_w_v, w_out, ln_g, ln_b):
    depth = w_in.shape[0]
    alpha = float((2 * depth) ** 0.25)
    lam_r, lam_i, wbs, wc, wcp, kd = _ssm_discretize(
        ssm_lam_re, ssm_lam_im, ssm_log_step, ssm_b_re, ssm_b_im, ssm_c_re, ssm_c_im)
    small = {
        "gm_w": gm_w_s,
        "gm_bias": jnp.repeat(jnp.swapaxes(gm_b_s, 1, 2), GM_HEAD_DIM, axis=2),
        "gm_g": gm_ln_g.reshape(depth, GM_WIDTH), "gm_b": gm_ln_b.reshape(depth, GM_WIDTH),
        "lam_r": lam_r, "lam_i": lam_i, "d": ssm_d, "glu_b": glu_b,
        "ln_g": ln_g, "ln_b": ln_b,
    }
    w_in16, w_out16 = w_in.astype(BF16), w_out.astype(BF16)
    glu_w16 = glu_w.astype(BF16)
    kt, vv = _kv_project(mem, jnp.swapaxes(xa_w_k, 1, 2).astype(BF16), xa_w_v.astype(BF16))
    for l in range(depth):
        big = {"w_in": w_in16[l], "w_out": w_out16[l], "wbs": wbs[l], "wc": wc[l],
               "wcp": wcp[l], "kd": kd[l], "glu_w": glu_w16[l]}
        x = _layer(x, l, kt, vv, big, small, alpha)
    return x
```

```python
import functools
import math

import jax
import jax.numpy as jnp
import numpy as np
from jax import lax
from jax.experimental import pallas as pl
from jax.experimental.pallas import tpu as pltpu

F32 = jnp.float32
BF16 = jnp.bfloat16

D_MODEL = 1024
GM_WIDTH = 512
GM_HEADS = 4
GM_HEAD_DIM = 128
CHUNK = 128
SSM_WIDTH = 256
SSM_GROUP = 16
SSM_GROUPS = 16
SSM_STATE = 64
N_STATE = SSM_GROUPS * SSM_STATE
XA_WIDTH = 256
XA_HEADS = 4
XA_HEAD_DIM = 64
IN_WIDTH = 3 * GM_WIDTH + 2 * SSM_WIDTH + 2 * XA_WIDTH
LN_EPS = 1e-5

C_U, C_V, C_GA = 0, GM_WIDTH, 2 * GM_WIDTH
C_XB = 3 * GM_WIDTH
C_GB = C_XB + SSM_WIDTH
C_Q = C_GB + SSM_WIDTH
C_GX = C_Q + XA_WIDTH
Y_A, Y_B, Y_X = 0, GM_WIDTH, GM_WIDTH + SSM_WIDTH

BLOCK_B = 8
BLOCK_T = CHUNK
ROWS = BLOCK_B * BLOCK_T
ROWS_PER_ITER = 2
S5_CHUNKS = 4
S5_STRIDE = 2
KD_OFFSETS = tuple(SSM_WIDTH * j * (j + 1) // 2 for j in range(S5_STRIDE - 1))
KD_ROWS = SSM_WIDTH * S5_STRIDE * (S5_STRIDE - 1) // 2
KV_BLOCK_B = 4
LANES = 128
PERM_PITCH = BLOCK_T + 8
VMEM_LIMIT_BYTES = 60 * 1024 * 1024

_LOG2E = 1.0 / math.log(2.0)
_GELU_K1 = -2.0 * math.sqrt(2.0 / math.pi) * _LOG2E
_GELU_K3 = _GELU_K1 * 0.044715


def _gelu(x):
    return x * (1.0 / (1.0 + jnp.exp2(x * (_GELU_K1 + _GELU_K3 * (x * x)))))


def _sigmoid(x):
    return 1.0 / (1.0 + jnp.exp2(x * (-_LOG2E)))


def _silu(x):
    return x * _sigmoid(x)


def _dot(a, b):
    return jnp.dot(a, b, preferred_element_type=F32)


def _aligned(v, m):
    return v if isinstance(v, int) else pl.multiple_of(v, m)


def _kv_kernel(mem_ref, wkt_ref, wv_ref, kt_ref, v_ref):
    depth = wkt_ref.shape[0]
    for b in range(KV_BLOCK_B):
        m16 = mem_ref[b].astype(BF16)
        for l in range(depth):
            kt = lax.dot_general(wkt_ref[l], m16, (((1,), (1,)), ((), ())),
                                 preferred_element_type=F32)
            kt_ref[l, b] = kt.astype(BF16)
            v_ref[l, b] = _dot(m16, wv_ref[l]).astype(BF16)


def _layer_kernel(alpha, layer,
                  x_ref, xn_ref, kt_ref, v_ref, w_in_ref, w_out_ref, wbs_ref, wc_ref, wcp_ref,
                  kd_ref, gluw_ref,
                  gmw_ref, gmbias_ref, gmg_ref, gmb_ref, lamr_ref, lami_ref, d_ref,
                  glub_ref, lng_ref, lnb_ref,
                  o_ref,
                  z0_ref, z1_ref, wm_ref, y_ref, perm_ref, xbp_ref, hb_ref, gateb_ref):
    j = pl.program_id(1)
    lrow = slice(layer, layer + 1)
    z_refs = (z0_ref, z1_ref)
    assert len(z_refs) == ROWS_PER_ITER

    @pl.when(j == 0)
    def _():
        hb_ref[0:BLOCK_B, :] = jnp.zeros((BLOCK_B, 2 * N_STATE), F32)

    tri = (lax.broadcasted_iota(jnp.int32, (CHUNK, CHUNK), 0)
           >= lax.broadcasted_iota(jnp.int32, (CHUNK, CHUNK), 1))
    for hp in range(GM_HEADS // 2):
        wm_ref[hp] = jnp.concatenate(
            [jnp.where(tri, gmw_ref[2 * hp + i], 0.0).astype(BF16) for i in range(2)], axis=1)
    row_head = lax.broadcasted_iota(jnp.int32, (XA_WIDTH, XA_WIDTH), 0) // XA_HEAD_DIM
    col_head = lax.broadcasted_iota(jnp.int32, (XA_WIDTH, XA_WIDTH), 1) // XA_HEAD_DIM
    out_head = lax.broadcasted_iota(jnp.int32, (BLOCK_T, XA_WIDTH), 1) // XA_HEAD_DIM

    def in_proj_fills(row, z_ref, x_row=None):
        x16 = (x_ref[row] if x_row is None else x_row).astype(BF16)

        def chunk_uv():
            z_ref[:, C_U:C_GA] = _dot(x16, w_in_ref[:, C_U:C_GA])

        def chunk_ssm():
            zz = _dot(x16, w_in_ref[:, C_GA:C_Q])
            z_ref[:, C_GA:C_XB] = zz[:, 0:GM_WIDTH]
            xs = zz[:, C_XB - C_GA:C_GB - C_GA]
            p0 = _aligned(row * PERM_PITCH, 8)
            for s in range(SSM_WIDTH // LANES):
                perm_ref[s, pl.ds(p0, BLOCK_T), :] = xs[:, s * LANES:(s + 1) * LANES]
            r0 = _aligned(row * BLOCK_T, BLOCK_T)
            gateb_ref[pl.ds(r0, BLOCK_T), :] = _silu(zz[:, C_GB - C_GA:])

        def chunk_attn():
            z_ref[:, C_Q:] = _dot(x16, w_in_ref[:, C_Q:])

        return chunk_uv, chunk_ssm, chunk_attn

    def branches(row, z_ref, fills, early_fills):
        r0 = _aligned(row * BLOCK_T, BLOCK_T)
        q16 = (z_ref[:, C_Q:C_GX] * (XA_HEAD_DIM ** -0.5 * _LOG2E)).astype(BF16)
        kt = kt_ref[row]
        vv = v_ref[row]
        scores = [_dot(q16, jnp.where(row_head == h, kt, jnp.zeros_like(kt)))
                  for h in range(XA_HEADS)]
        for f in fills[:early_fills]:
            f()
        o = None
        scale = None
        for h in range(XA_HEADS):
            s = scores[h]
            p = jnp.exp2(s - jnp.max(s, axis=-1, keepdims=True))
            inv = 1.0 / jnp.sum(p, axis=-1, keepdims=True)
            oh = _dot(p.astype(BF16), jnp.where(col_head == h, vv, jnp.zeros_like(vv)))
            o = oh if o is None else o + oh
            scale = inv if scale is None else jnp.where(out_head == h, inv, scale)
        v = _gelu(z_ref[:, C_V:C_GA])
        vn16 = []
        for h in range(GM_HEADS):
            sl = slice(h * GM_HEAD_DIM, (h + 1) * GM_HEAD_DIM)
            vh = v[:, sl]
            mu = jnp.mean(vh, axis=-1, keepdims=True)
            xc = vh - mu
            var = jnp.mean(xc * xc, axis=-1, keepdims=True)
            vn = xc * lax.rsqrt(var + LN_EPS) * gmg_ref[lrow, sl] + gmb_ref[lrow, sl]
            vn16.append(vn.astype(BF16))
        mixed = []
        zero = jnp.zeros((CHUNK, GM_HEAD_DIM), BF16)
        for hp in range(GM_HEADS // 2):
            v_a, v_b = vn16[2 * hp], vn16[2 * hp + 1]
            rhs = jnp.concatenate([jnp.concatenate([v_a, zero], axis=1),
                                   jnp.concatenate([zero, v_b], axis=1)], axis=0)
            mixed.append(_dot(wm_ref[hp], rhs))
        for f in fills[early_fills:]:
            f()
        u = _gelu(z_ref[:, C_U:C_V])
        ya = u * (jnp.concatenate(mixed, axis=1) + gmbias_ref[...]) * _silu(z_ref[:, C_GA:C_XB])
        y_ref[pl.ds(r0, BLOCK_T), Y_A:Y_B] = ya.astype(BF16)
        yx = o * scale * _silu(z_ref[:, C_GX:])
        y_ref[pl.ds(r0, BLOCK_T), Y_X:] = yx.astype(BF16)

    n_z = len(z_refs)

    @pl.when((pl.program_id(0) == 0) & (j == 0))
    def _():
        for f in in_proj_fills(0, z_refs[0]):
            f()

    def row_group(g, carry):
        k0 = ROWS_PER_ITER * g
        for i in range(ROWS_PER_ITER):
            branches(k0 + i, z_refs[i % n_z], in_proj_fills(k0 + i + 1, z_refs[(i + 1) % n_z]), 3)
        return carry

    n_looped = (BLOCK_B - 1) // ROWS_PER_ITER * ROWS_PER_ITER
    lax.fori_loop(0, n_looped // ROWS_PER_ITER, row_group, 0)
    for k in range(n_looped, BLOCK_B - 1):
        branches(k, z_refs[k % n_z], in_proj_fills(k + 1, z_refs[(k + 1) % n_z]), 2)

    n_slab = SSM_WIDTH // LANES
    sup_steps = BLOCK_T // S5_STRIDE
    sup_rows = sup_steps * BLOCK_B

    for t in range(BLOCK_T):
        tp, jj = divmod(t, S5_STRIDE)
        for s in range(n_slab):
            lane0 = jj * SSM_WIDTH + s * LANES
            xbp_ref[tp * BLOCK_B:(tp + 1) * BLOCK_B, lane0:lane0 + LANES] = (
                perm_ref[s, pl.ds(t, BLOCK_B, stride=PERM_PITCH), :])

    def g_proj(ri):
        def f():
            hb_ref[BLOCK_B:BLOCK_B + sup_rows, ri * N_STATE:(ri + 1) * N_STATE] = _dot(
                xbp_ref[...].astype(BF16), wbs_ref[ri])
        return f

    branches(BLOCK_B - 1, z_refs[(BLOCK_B - 1) % n_z], (g_proj(0), g_proj(1)), 2)

    lam_r = jnp.broadcast_to(lamr_ref[lrow, :], (BLOCK_B, N_STATE))
    lam_i = jnp.broadcast_to(lami_ref[lrow, :], (BLOCK_B, N_STATE))

    steps = sup_steps // S5_CHUNKS
    chunk = steps * BLOCK_B
    state = [hb_ref[0:BLOCK_B, 0:N_STATE], hb_ref[0:BLOCK_B, N_STATE:]]

    def scan_chunk(c):
        hr, hi = state
        for tp in range(c * steps, (c + 1) * steps):
            rows = slice((tp + 1) * BLOCK_B, (tp + 2) * BLOCK_B)
            gr = hb_ref[rows, 0:N_STATE]
            gi = hb_ref[rows, N_STATE:]
            hr, hi = lam_r * hr - lam_i * hi + gr, lam_r * hi + lam_i * hr + gi
            hb_ref[rows, 0:N_STATE] = hr
            hb_ref[rows, N_STATE:] = hi
        state[0], state[1] = hr, hi

    def c_proj(c):
        r0 = c * chunk
        x_c = xbp_ref[r0:r0 + chunk, :]
        d = d_ref[lrow, :]
        h_prev = hb_ref[r0:r0 + chunk, :].astype(BF16)
        ys = []
        for jj in range(S5_STRIDE - 1):
            k0 = KD_OFFSETS[jj]
            n_in = (jj + 1) * SSM_WIDTH
            ys.append(_dot(h_prev, wcp_ref[jj])
                      + _dot(x_c[:, 0:n_in].astype(BF16), kd_ref[k0:k0 + n_in, :])
                      + d * x_c[:, jj * SSM_WIDTH:(jj + 1) * SSM_WIDTH])
        h_last = hb_ref[BLOCK_B + r0:BLOCK_B + r0 + chunk, :].astype(BF16)
        ys.append(_dot(h_last, wc_ref[...]) + d * x_c[:, (S5_STRIDE - 1) * SSM_WIDTH:])
        return ys

    def glu(c, ys):
        r0 = c * chunk
        for jj, y in enumerate(ys):
            yg = _gelu(y)
            xbp_ref[r0:r0 + chunk, jj * SSM_WIDTH:(jj + 1) * SSM_WIDTH] = (
                yg * _sigmoid(_dot(yg.astype(BF16), gluw_ref[...]) + glub_ref[lrow, :]))

    ys_prev = None
    for c in range(S5_CHUNKS):
        scan_chunk(c)
        ys_c = c_proj(c)
        if ys_prev is not None:
            glu(c - 1, ys_prev)
        ys_prev = ys_c
    glu(S5_CHUNKS - 1, ys_prev)
    hb_ref[0:BLOCK_B, 0:N_STATE] = state[0]
    hb_ref[0:BLOCK_B, N_STATE:] = state[1]

    for t in range(BLOCK_T):
        tp, jj = divmod(t, S5_STRIDE)
        for s in range(n_slab):
            lane0 = jj * SSM_WIDTH + s * LANES
            perm_ref[s, pl.ds(t, BLOCK_B, stride=PERM_PITCH), :] = (
                xbp_ref[tp * BLOCK_B:(tp + 1) * BLOCK_B, lane0:lane0 + LANES])

    for b in range(BLOCK_B):
        ybb = jnp.concatenate(
            [perm_ref[s, b * PERM_PITCH:b * PERM_PITCH + BLOCK_T, :] for s in range(n_slab)],
            axis=1)
        rows = slice(b * BLOCK_T, (b + 1) * BLOCK_T)
        y_ref[rows, Y_B:Y_X] = (ybb * gateb_ref[rows, :]).astype(BF16)

    for b in range(BLOCK_B):
        rows = slice(b * BLOCK_T, (b + 1) * BLOCK_T)
        r = alpha * x_ref[b] + _dot(y_ref[rows, :], w_out_ref[...])
        if b == BLOCK_B - 1:
            for f in in_proj_fills(0, z_refs[0], x_row=xn_ref[0]):
                f()
        mu = jnp.mean(r, axis=-1, keepdims=True)
        rc = r - mu
        var = jnp.mean(rc * rc, axis=-1, keepdims=True)
        o_ref[b] = rc * lax.rsqrt(var + LN_EPS) * lng_ref[lrow, :] + lnb_ref[lrow, :]


def _full(shape):
    nd = len(shape)
    return pl.BlockSpec(shape, lambda i, j, _nd=nd: (0,) * _nd)


def _of_layer(shape, layer):
    nd = len(shape)
    return pl.BlockSpec((None,) + tuple(shape),
                        lambda i, j, _l=layer, _nd=nd: (_l,) + (0,) * _nd)


def _kv_project(mem, wkt16, wv16):
    bsz, m_len, d = mem.shape
    depth = wkt16.shape[0]
    kt_sds = jax.ShapeDtypeStruct((depth, bsz, XA_WIDTH, m_len), BF16)
    v_sds = jax.ShapeDtypeStruct((depth, bsz, m_len, XA_WIDTH), BF16)
    return pl.pallas_call(
        _kv_kernel,
        grid=(bsz // KV_BLOCK_B,),
        in_specs=[pl.BlockSpec((KV_BLOCK_B, m_len, d), lambda i: (i, 0, 0)),
                  pl.BlockSpec((depth, XA_WIDTH, d), lambda i: (0, 0, 0)),
                  pl.BlockSpec((depth, d, XA_WIDTH), lambda i: (0, 0, 0))],
        out_specs=[pl.BlockSpec((depth, KV_BLOCK_B, XA_WIDTH, m_len), lambda i: (0, i, 0, 0)),
                   pl.BlockSpec((depth, KV_BLOCK_B, m_len, XA_WIDTH), lambda i: (0, i, 0, 0))],
        out_shape=[kt_sds, v_sds],
        name="kv_project",
    )(mem, wkt16, wv16)


def _layer(x, layer, kt, vv, big, small, alpha):
    bsz, seq, d = x.shape
    depth, _, _, m_len = kt.shape
    grid = (bsz // BLOCK_B, seq // BLOCK_T)
    n_steps = grid[0] * grid[1]

    def next_row0(i, j):
        s = jnp.minimum(i * grid[1] + j + 1, n_steps - 1)
        return (s // grid[1]) * BLOCK_B, s % grid[1], 0

    in_specs = [
        pl.BlockSpec((BLOCK_B, BLOCK_T, d), lambda i, j: (i, j, 0)),
        pl.BlockSpec((1, BLOCK_T, d), next_row0),
        pl.BlockSpec((None, BLOCK_B, XA_WIDTH, m_len), lambda i, j: (layer, i, 0, 0)),
        pl.BlockSpec((None, BLOCK_B, m_len, XA_WIDTH), lambda i, j: (layer, i, 0, 0)),
        _of_layer((d, IN_WIDTH), layer), _of_layer((d, d), layer),
        _of_layer((2, S5_STRIDE * SSM_WIDTH, N_STATE), layer), _full((2 * N_STATE, SSM_WIDTH)),
        _full((S5_STRIDE - 1, 2 * N_STATE, SSM_WIDTH)), _full((KD_ROWS, SSM_WIDTH)),
        _full((SSM_WIDTH, SSM_WIDTH)),
        _of_layer((GM_HEADS, CHUNK, CHUNK), layer),
        _of_layer((CHUNK, GM_WIDTH), layer),
        _full((depth, GM_WIDTH)), _full((depth, GM_WIDTH)),
        _full((depth, N_STATE)), _full((depth, N_STATE)),
        _full((depth, SSM_WIDTH)), _full((depth, SSM_WIDTH)),
        _full((depth, d)), _full((depth, d)),
    ]
    scratch = [
        pltpu.VMEM((BLOCK_T, IN_WIDTH), F32),
        pltpu.VMEM((BLOCK_T, IN_WIDTH), F32),
        pltpu.VMEM((GM_HEADS // 2, CHUNK, 2 * CHUNK), BF16),
        pltpu.VMEM((ROWS, d), BF16),
        pltpu.VMEM((SSM_WIDTH // LANES, BLOCK_B * PERM_PITCH, LANES), F32),
        pltpu.VMEM((ROWS // S5_STRIDE, S5_STRIDE * SSM_WIDTH), F32),
        pltpu.VMEM((ROWS // S5_STRIDE + BLOCK_B, 2 * N_STATE), F32),
        pltpu.VMEM((ROWS, SSM_WIDTH), F32),
    ]
    return pl.pallas_call(
        functools.partial(_layer_kernel, alpha, layer),
        grid=grid,
        in_specs=in_specs,
        out_specs=pl.BlockSpec((BLOCK_B, BLOCK_T, d), lambda i, j: (i, j, 0)),
        out_shape=jax.ShapeDtypeStruct(x.shape, x.dtype),
        scratch_shapes=scratch,
        compiler_params=pltpu.CompilerParams(
            dimension_semantics=("arbitrary", "arbitrary"),
            vmem_limit_bytes=VMEM_LIMIT_BYTES),
        name="trunk_layer",
    )(x, x, kt, vv, big["w_in"], big["w_out"], big["wbs"], big["wc"], big["wcp"], big["kd"],
      big["glu_w"],
      small["gm_w"], small["gm_bias"], small["gm_g"], small["gm_b"], small["lam_r"],
      small["lam_i"], small["d"], small["glu_b"], small["ln_g"], small["ln_b"])


def _ssm_discretize(lam_re, lam_im, log_step, b_re, b_im, c_re, c_im):
    depth = lam_re.shape[0]
    step = jnp.exp(log_step)[:, :, None]
    a, b = lam_re * step, lam_im * step

    def lam_pow(m):
        mag = jnp.exp(m * a)
        return mag * jnp.cos(m * b), mag * jnp.sin(m * b)

    def cmul(xr, xi, yr, yi):
        return xr * yr - xi * yi, xr * yi + xi * yr

    def group_block_diag(m, per_row, per_col):
        rows, cols = m.shape[-2], SSM_GROUPS * per_col
        spread = (np.arange(per_col)[:, None] == np.arange(cols)[None, :] % per_col)
        tiled = jnp.einsum('...p,pq->...q', m, spread.astype(np.float32),
                           precision=lax.Precision.HIGHEST, preferred_element_type=F32)
        same_group = ((np.arange(rows)[:, None] // per_row) % SSM_GROUPS
                      == np.arange(cols)[None, :] // per_col)
        return jnp.where(same_group, tiled.astype(BF16), jnp.zeros((), BF16))

    lbr, lbi = lam_pow(1)
    den = lam_re * lam_re + lam_im * lam_im
    fr = ((lbr - 1.0) * lam_re + lbi * lam_im) / den
    fi = (lbi * lam_re - (lbr - 1.0) * lam_im) / den
    bbr, bbi = cmul(fr[..., None], fi[..., None], b_re, b_im)

    coef = [cmul(*(q[..., None] for q in lam_pow(S5_STRIDE - 1 - j)), bbr, bbi)
            for j in range(S5_STRIDE)]
    w_in_side = jnp.stack([jnp.stack([c[ri] for c in coef], axis=1) for ri in range(2)],
                          axis=1)
    w_in_side = jnp.transpose(w_in_side, (0, 1, 2, 3, 5, 4)).reshape(
        depth, 2, S5_STRIDE * SSM_WIDTH, SSM_STATE)
    wbs = group_block_diag(w_in_side, SSM_GROUP, SSM_STATE)

    def c_times_lam_pow(m):
        pr, pi = lam_pow(m)
        return cmul(c_re, c_im, pr[:, :, None, :], pi[:, :, None, :])

    c_pows = [c_times_lam_pow(m) for m in range(S5_STRIDE)]
    w_out_side = jnp.stack([jnp.stack([cr, -ci], axis=1) for cr, ci in c_pows], axis=1)
    w_out_side = jnp.transpose(w_out_side, (0, 1, 2, 3, 5, 4)).reshape(
        depth, S5_STRIDE, 2, N_STATE, SSM_GROUP)
    wc_all = group_block_diag(w_out_side, SSM_STATE, SSM_GROUP).reshape(
        depth, S5_STRIDE, 2 * N_STATE, SSM_WIDTH)
    wc, wcp = wc_all[:, 0], wc_all[:, 1:]

    k_dir = jnp.stack([jnp.einsum('lgop,lgpi->lgio', cr, bbr)
                       - jnp.einsum('lgop,lgpi->lgio', ci, bbi)
                       for cr, ci in c_pows[:S5_STRIDE - 1]], axis=1)
    k_dir = group_block_diag(
        k_dir.reshape(depth, S5_STRIDE - 1, SSM_WIDTH, SSM_GROUP), SSM_GROUP, SSM_GROUP)
    kd = jnp.concatenate([k_dir[:, j - i] for j in range(S5_STRIDE - 1) for i in range(j + 1)],
                         axis=1)
    lsr, lsi = lam_pow(S5_STRIDE)
    return lsr.reshape(depth, N_STATE), lsi.reshape(depth, N_STATE), wbs, wc, wcp, kd


def kernel(x, mem, w_in, gm_w_s, gm_b_s, gm_ln_g, gm_ln_b, ssm_lam_re, ssm_lam_im,
           ssm_log_step, ssm_b_re, ssm_b_im, ssm_c_re, ssm_c_im, ssm_d, glu_w, glu_b,
           xa_w_k, xa_w_v, w_out, ln_g, ln_b):
    depth = w_in.shape[0]
    alpha = float((2 * depth) ** 0.25)
    lam_r, lam_i, wbs, wc, wcp, kd = _ssm_discretize(
        ssm_lam_re, ssm_lam_im, ssm_log_step, ssm_b_re, ssm_b_im, ssm_c_re, ssm_c_im)
    small = {
        "gm_w": gm_w_s,
        "gm_bias": jnp.repeat(jnp.swapaxes(gm_b_s, 1, 2), GM_HEAD_DIM, axis=2),
        "gm_g": gm_ln_g.reshape(depth, GM_WIDTH), "gm_b": gm_ln_b.reshape(depth, GM_WIDTH),
        "lam_r": lam_r, "lam_i": lam_i, "d": ssm_d, "glu_b": glu_b,
        "ln_g": ln_g, "ln_b": ln_b,
    }
    w_in16, w_out16 = w_in.astype(BF16), w_out.astype(BF16)
    glu_w16 = glu_w.astype(BF16)
    kt, vv = _kv_project(mem, jnp.swapaxes(xa_w_k, 1, 2).astype(BF16), xa_w_v.astype(BF16))
    for l in range(depth):
        big = {"w_in": w_in16, "w_out": w_out16, "wbs": wbs, "wc": wc[l],
               "wcp": wcp[l], "kd": kd[l], "glu_w": glu_w16[l]}
        x = _layer(x, l, kt, vv, big, small, alpha)
    return x
```

```python
import functools
import math

import jax
import jax.numpy as jnp
import numpy as np
from jax import lax
from jax.experimental import pallas as pl
from jax.experimental.pallas import tpu as pltpu

F32 = jnp.float32
BF16 = jnp.bfloat16

D_MODEL = 1024
GM_WIDTH = 512
GM_HEADS = 4
GM_HEAD_DIM = 128
CHUNK = 128
SSM_WIDTH = 256
SSM_GROUP = 16
SSM_GROUPS = 16
SSM_STATE = 64
N_STATE = SSM_GROUPS * SSM_STATE
XA_WIDTH = 256
XA_HEADS = 4
XA_HEAD_DIM = 64
IN_WIDTH = 3 * GM_WIDTH + 2 * SSM_WIDTH + 2 * XA_WIDTH
LN_EPS = 1e-5

C_U, C_V, C_GA = 0, GM_WIDTH, 2 * GM_WIDTH
C_XB = 3 * GM_WIDTH
C_GB = C_XB + SSM_WIDTH
C_Q = C_GB + SSM_WIDTH
C_GX = C_Q + XA_WIDTH
Y_A, Y_B, Y_X = 0, GM_WIDTH, GM_WIDTH + SSM_WIDTH

BLOCK_B = 8
BLOCK_T = CHUNK
ROWS = BLOCK_B * BLOCK_T
ROWS_PER_ITER = 2
S5_CHUNKS = 4
S5_STRIDE = 2
KD_OFFSETS = tuple(SSM_WIDTH * j * (j + 1) // 2 for j in range(S5_STRIDE - 1))
KD_ROWS = SSM_WIDTH * S5_STRIDE * (S5_STRIDE - 1) // 2
KV_BLOCK_B = 4
LANES = 128
PERM_PITCH = BLOCK_T + 8
VMEM_LIMIT_BYTES = 60 * 1024 * 1024

_LOG2E = 1.0 / math.log(2.0)
_GELU_K1 = -2.0 * math.sqrt(2.0 / math.pi) * _LOG2E
_GELU_K3 = _GELU_K1 * 0.044715


def _gelu(x):
    return x * (1.0 / (1.0 + jnp.exp2(x * (_GELU_K1 + _GELU_K3 * (x * x)))))


def _sigmoid(x):
    return 1.0 / (1.0 + jnp.exp2(x * (-_LOG2E)))


def _silu(x):
    return x * _sigmoid(x)


def _dot(a, b):
    return jnp.dot(a, b, preferred_element_type=F32)


def _aligned(v, m):
    return v if isinstance(v, int) else pl.multiple_of(v, m)


def _kv_kernel(mem_ref, wkt_ref, wv_ref, kt_ref, v_ref):
    depth = wkt_ref.shape[0]
    for b in range(KV_BLOCK_B):
        m16 = mem_ref[b].astype(BF16)
        for l in range(depth):
            kt = lax.dot_general(wkt_ref[l], m16, (((1,), (1,)), ((), ())),
                                 preferred_element_type=F32)
            kt_ref[l, b] = kt.astype(BF16)
            v_ref[l, b] = _dot(m16, wv_ref[l]).astype(BF16)


def _layer_kernel(alpha, layer,
                  x_ref, xn_ref, kt_ref, v_ref, w_in_ref, w_out_ref, wbs_ref, wc_ref, wcp_ref,
                  kd_ref, gluw_ref,
                  gmw_ref, gmbias_ref, gmg_ref, gmb_ref, lamr_ref, lami_ref, d_ref,
                  glub_ref, lng_ref, lnb_ref,
                  o_ref,
                  z0_ref, z1_ref, wm_ref, y_ref, perm_ref, xbp_ref, hb_ref, gateb_ref):
    j = pl.program_id(1)
    lrow = slice(layer, layer + 1)
    z_refs = (z0_ref, z1_ref)
    assert len(z_refs) == ROWS_PER_ITER

    @pl.when(j == 0)
    def _():
        hb_ref[0:BLOCK_B, :] = jnp.zeros((BLOCK_B, 2 * N_STATE), F32)

    tri = (lax.broadcasted_iota(jnp.int32, (CHUNK, CHUNK), 0)
           >= lax.broadcasted_iota(jnp.int32, (CHUNK, CHUNK), 1))
    for hp in range(GM_HEADS // 2):
        wm_ref[hp] = jnp.concatenate(
            [jnp.where(tri, gmw_ref[2 * hp + i], 0.0).astype(BF16) for i in range(2)], axis=1)
    row_head = lax.broadcasted_iota(jnp.int32, (XA_WIDTH, XA_WIDTH), 0) // XA_HEAD_DIM
    col_head = lax.broadcasted_iota(jnp.int32, (XA_WIDTH, XA_WIDTH), 1) // XA_HEAD_DIM
    out_head = lax.broadcasted_iota(jnp.int32, (BLOCK_T, XA_WIDTH), 1) // XA_HEAD_DIM

    def in_proj_fills(row, z_ref, x_row=None):
        x16 = (x_ref[row] if x_row is None else x_row).astype(BF16)

        def chunk_uv():
            z_ref[:, C_U:C_GA] = _dot(x16, w_in_ref[:, C_U:C_GA])

        def chunk_ssm():
            zz = _dot(x16, w_in_ref[:, C_GA:C_Q])
            z_ref[:, C_GA:C_XB] = zz[:, 0:GM_WIDTH]
            xs = zz[:, C_XB - C_GA:C_GB - C_GA]
            p0 = _aligned(row * PERM_PITCH, 8)
            for s in range(SSM_WIDTH // LANES):
                perm_ref[s, pl.ds(p0, BLOCK_T), :] = xs[:, s * LANES:(s + 1) * LANES]
            r0 = _aligned(row * BLOCK_T, BLOCK_T)
            gateb_ref[pl.ds(r0, BLOCK_T), :] = _silu(zz[:, C_GB - C_GA:])

        def chunk_attn():
            z_ref[:, C_Q:] = _dot(x16, w_in_ref[:, C_Q:])

        return chunk_uv, chunk_ssm, chunk_attn

    def branches(row, z_ref, fills, early_fills):
        r0 = _aligned(row * BLOCK_T, BLOCK_T)
        q16 = (z_ref[:, C_Q:C_GX] * (XA_HEAD_DIM ** -0.5 * _LOG2E)).astype(BF16)
        kt = kt_ref[row]
        vv = v_ref[row]
        scores = [_dot(q16, jnp.where(row_head == h, kt, jnp.zeros_like(kt)))
                  for h in range(XA_HEADS)]
        for f in fills[:early_fills]:
            f()
        o = None
        scale = None
        for h in range(XA_HEADS):
            s = scores[h]
            p = jnp.exp2(s - jnp.max(s, axis=-1, keepdims=True))
            inv = 1.0 / jnp.sum(p, axis=-1, keepdims=True)
            oh = _dot(p.astype(BF16), jnp.where(col_head == h, vv, jnp.zeros_like(vv)))
            o = oh if o is None else o + oh
            scale = inv if scale is None else jnp.where(out_head == h, inv, scale)
        v = _gelu(z_ref[:, C_V:C_GA])
        vn16 = []
        for h in range(GM_HEADS):
            sl = slice(h * GM_HEAD_DIM, (h + 1) * GM_HEAD_DIM)
            vh = v[:, sl]
            mu = jnp.mean(vh, axis=-1, keepdims=True)
            xc = vh - mu
            var = jnp.mean(xc * xc, axis=-1, keepdims=True)
            vn = xc * lax.rsqrt(var + LN_EPS) * gmg_ref[lrow, sl] + gmb_ref[lrow, sl]
            vn16.append(vn.astype(BF16))
        mixed = []
        zero = jnp.zeros((CHUNK, GM_HEAD_DIM), BF16)
        for hp in range(GM_HEADS // 2):
            v_a, v_b = vn16[2 * hp], vn16[2 * hp + 1]
            rhs = jnp.concatenate([jnp.concatenate([v_a, zero], axis=1),
                                   jnp.concatenate([zero, v_b], axis=1)], axis=0)
            mixed.append(_dot(wm_ref[hp], rhs))
        for f in fills[early_fills:]:
            f()
        u = _gelu(z_ref[:, C_U:C_V])
        ya = u * (jnp.concatenate(mixed, axis=1) + gmbias_ref[...]) * _silu(z_ref[:, C_GA:C_XB])
        y_ref[pl.ds(r0, BLOCK_T), Y_A:Y_B] = ya.astype(BF16)
        yx = o * scale * _silu(z_ref[:, C_GX:])
        y_ref[pl.ds(r0, BLOCK_T), Y_X:] = yx.astype(BF16)

    n_z = len(z_refs)

    @pl.when((pl.program_id(0) == 0) & (j == 0))
    def _():
        for f in in_proj_fills(0, z_refs[0]):
            f()

    def row_group(g, carry):
        k0 = ROWS_PER_ITER * g
        for i in range(ROWS_PER_ITER):
            branches(k0 + i, z_refs[i % n_z], in_proj_fills(k0 + i + 1, z_refs[(i + 1) % n_z]), 3)
        return carry

    n_looped = (BLOCK_B - 1) // ROWS_PER_ITER * ROWS_PER_ITER
    lax.fori_loop(0, n_looped // ROWS_PER_ITER, row_group, 0)
    for k in range(n_looped, BLOCK_B - 1):
        branches(k, z_refs[k % n_z], in_proj_fills(k + 1, z_refs[(k + 1) % n_z]), 2)

    n_slab = SSM_WIDTH // LANES
    sup_steps = BLOCK_T // S5_STRIDE
    sup_rows = sup_steps * BLOCK_B

    for t in range(BLOCK_T):
        tp, jj = divmod(t, S5_STRIDE)
        for s in range(n_slab):
            lane0 = jj * SSM_WIDTH + s * LANES
            xbp_ref[tp * BLOCK_B:(tp + 1) * BLOCK_B, lane0:lane0 + LANES] = (
                perm_ref[s, pl.ds(t, BLOCK_B, stride=PERM_PITCH), :])

    def g_proj(ri):
        def f():
            hb_ref[BLOCK_B:BLOCK_B + sup_rows, ri * N_STATE:(ri + 1) * N_STATE] = _dot(
                xbp_ref[...].astype(BF16), wbs_ref[ri])
        return f

    branches(BLOCK_B - 1, z_refs[(BLOCK_B - 1) % n_z], (g_proj(0), g_proj(1)), 2)

    lam_r = jnp.broadcast_to(lamr_ref[lrow, :], (BLOCK_B, N_STATE))
    lam_i = jnp.broadcast_to(lami_ref[lrow, :], (BLOCK_B, N_STATE))

    steps = sup_steps // S5_CHUNKS
    chunk = steps * BLOCK_B
    state = [hb_ref[0:BLOCK_B, 0:N_STATE], hb_ref[0:BLOCK_B, N_STATE:]]

    def scan_chunk(c):
        hr, hi = state
        for tp in range(c * steps, (c + 1) * steps):
            rows = slice((tp + 1) * BLOCK_B, (tp + 2) * BLOCK_B)
            gr = hb_ref[rows, 0:N_STATE]
            gi = hb_ref[rows, N_STATE:]
            hr, hi = lam_r * hr - lam_i * hi + gr, lam_r * hi + lam_i * hr + gi
            hb_ref[rows, 0:N_STATE] = hr
            hb_ref[rows, N_STATE:] = hi
        state[0], state[1] = hr, hi

    def c_proj(c):
        r0 = c * chunk
        x_c = xbp_ref[r0:r0 + chunk, :]
        d = d_ref[lrow, :]
        h_prev = hb_ref[r0:r0 + chunk, :].astype(BF16)
        ys = []
        for jj in range(S5_STRIDE - 1):
            k0 = KD_OFFSETS[jj]
            n_in = (jj + 1) * SSM_WIDTH
            ys.append(_dot(h_prev, wcp_ref[jj])
                      + _dot(x_c[:, 0:n_in].astype(BF16), kd_ref[k0:k0 + n_in, :])
                      + d * x_c[:, jj * SSM_WIDTH:(jj + 1) * SSM_WIDTH])
        h_last = hb_ref[BLOCK_B + r0:BLOCK_B + r0 + chunk, :].astype(BF16)
        ys.append(_dot(h_last, wc_ref[...]) + d * x_c[:, (S5_STRIDE - 1) * SSM_WIDTH:])
        return ys

    def glu(c, ys):
        r0 = c * chunk
        for jj, y in enumerate(ys):
            yg = _gelu(y)
            xbp_ref[r0:r0 + chunk, jj * SSM_WIDTH:(jj + 1) * SSM_WIDTH] = (
                yg * _sigmoid(_dot(yg.astype(BF16), gluw_ref[...]) + glub_ref[lrow, :]))

    ys_prev = None
    for c in range(S5_CHUNKS):
        scan_chunk(c)
        ys_c = c_proj(c)
        if ys_prev is not None:
            glu(c - 1, ys_prev)
        ys_prev = ys_c
    glu(S5_CHUNKS - 1, ys_prev)
    hb_ref[0:BLOCK_B, 0:N_STATE] = state[0]
    hb_ref[0:BLOCK_B, N_STATE:] = state[1]

    for t in range(BLOCK_T):
        tp, jj = divmod(t, S5_STRIDE)
        for s in range(n_slab):
            lane0 = jj * SSM_WIDTH + s * LANES
            perm_ref[s, pl.ds(t, BLOCK_B, stride=PERM_PITCH), :] = (
                xbp_ref[tp * BLOCK_B:(tp + 1) * BLOCK_B, lane0:lane0 + LANES])

    for b in range(BLOCK_B):
        ybb = jnp.concatenate(
            [perm_ref[s, b * PERM_PITCH:b * PERM_PITCH + BLOCK_T, :] for s in range(n_slab)],
            axis=1)
        rows = slice(b * BLOCK_T, (b + 1) * BLOCK_T)
        y_ref[rows, Y_B:Y_X] = (ybb * gateb_ref[rows, :]).astype(BF16)

    for b in range(BLOCK_B):
        rows = slice(b * BLOCK_T, (b + 1) * BLOCK_T)
        r = alpha * x_ref[b] + _dot(y_ref[rows, :], w_out_ref[...])
        if b == BLOCK_B - 1:
            for f in in_proj_fills(0, z_refs[0], x_row=xn_ref[0]):
                f()
        mu = jnp.mean(r, axis=-1, keepdims=True)
        rc = r - mu
        var = jnp.mean(rc * rc, axis=-1, keepdims=True)
        o_ref[b] = rc * lax.rsqrt(var + LN_EPS) * lng_ref[lrow, :] + lnb_ref[lrow, :]


def _full(shape):
    nd = len(shape)
    return pl.BlockSpec(shape, lambda i, j, _nd=nd: (0,) * _nd)


def _of_layer(shape, layer):
    nd = len(shape)
    return pl.BlockSpec((None,) + tuple(shape),
                        lambda i, j, _l=layer, _nd=nd: (_l,) + (0,) * _nd)


def _kv_project(mem, wkt16, wv16):
    bsz, m_len, d = mem.shape
    depth = wkt16.shape[0]
    kt_sds = jax.ShapeDtypeStruct((depth, bsz, XA_WIDTH, m_len), BF16)
    v_sds = jax.ShapeDtypeStruct((depth, bsz, m_len, XA_WIDTH), BF16)
    return pl.pallas_call(
        _kv_kernel,
        grid=(bsz // KV_BLOCK_B,),
        in_specs=[pl.BlockSpec((KV_BLOCK_B, m_len, d), lambda i: (i, 0, 0)),
                  pl.BlockSpec((depth, XA_WIDTH, d), lambda i: (0, 0, 0)),
                  pl.BlockSpec((depth, d, XA_WIDTH), lambda i: (0, 0, 0))],
        out_specs=[pl.BlockSpec((depth, KV_BLOCK_B, XA_WIDTH, m_len), lambda i: (0, i, 0, 0)),
                   pl.BlockSpec((depth, KV_BLOCK_B, m_len, XA_WIDTH), lambda i: (0, i, 0, 0))],
        out_shape=[kt_sds, v_sds],
        name="kv_project",
    )(mem, wkt16, wv16)


def _layer(x, layer, kt, vv, big, small, alpha):
    bsz, seq, d = x.shape
    depth, _, _, m_len = kt.shape
    grid = (bsz // BLOCK_B, seq // BLOCK_T)
    n_steps = grid[0] * grid[1]

    def next_row0(i, j):
        s = jnp.minimum(i * grid[1] + j + 1, n_steps - 1)
        return (s // grid[1]) * BLOCK_B, s % grid[1], 0

    in_specs = [
        pl.BlockSpec((BLOCK_B, BLOCK_T, d), lambda i, j: (i, j, 0)),
        pl.BlockSpec((1, BLOCK_T, d), next_row0),
        pl.BlockSpec((None, BLOCK_B, XA_WIDTH, m_len), lambda i, j: (layer, i, 0, 0)),
        pl.BlockSpec((None, BLOCK_B, m_len, XA_WIDTH), lambda i, j: (layer, i, 0, 0)),
        _of_layer((d, IN_WIDTH), layer), _of_layer((d, d), layer),
        _of_layer((2, S5_STRIDE * SSM_WIDTH, N_STATE), layer), _full((2 * N_STATE, SSM_WIDTH)),
        _full((S5_STRIDE - 1, 2 * N_STATE, SSM_WIDTH)), _full((KD_ROWS, SSM_WIDTH)),
        _full((SSM_WIDTH, SSM_WIDTH)),
        _of_layer((GM_HEADS, CHUNK, CHUNK), layer),
        _of_layer((CHUNK, GM_WIDTH), layer),
        _full((depth, GM_WIDTH)), _full((depth, GM_WIDTH)),
        _full((depth, N_STATE)), _full((depth, N_STATE)),
        _full((depth, SSM_WIDTH)), _full((depth, SSM_WIDTH)),
        _full((depth, d)), _full((depth, d)),
    ]
    scratch = [
        pltpu.VMEM((BLOCK_T, IN_WIDTH), F32),
        pltpu.VMEM((BLOCK_T, IN_WIDTH), F32),
        pltpu.VMEM((GM_HEADS // 2, CHUNK, 2 * CHUNK), BF16),
        pltpu.VMEM((ROWS, d), BF16),
        pltpu.VMEM((SSM_WIDTH // LANES, BLOCK_B * PERM_PITCH, LANES), F32),
        pltpu.VMEM((ROWS // S5_STRIDE, S5_STRIDE * SSM_WIDTH), F32),
        pltpu.VMEM((ROWS // S5_STRIDE + BLOCK_B, 2 * N_STATE), F32),
        pltpu.VMEM((ROWS, SSM_WIDTH), F32),
    ]
    return pl.pallas_call(
        functools.partial(_layer_kernel, alpha, layer),
        grid=grid,
        in_specs=in_specs,
        out_specs=pl.BlockSpec((BLOCK_B, BLOCK_T, d), lambda i, j: (i, j, 0)),
        out_shape=jax.ShapeDtypeStruct(x.shape, x.dtype),
        scratch_shapes=scratch,
        compiler_params=pltpu.CompilerParams(
            dimension_semantics=("arbitrary", "arbitrary"),
            vmem_limit_bytes=VMEM_LIMIT_BYTES),
        name="trunk_layer",
    )(x, x, kt, vv, big["w_in"], big["w_out"], big["wbs"], big["wc"], big["wcp"], big["kd"],
      big["glu_w"],
      small["gm_w"], small["gm_bias"], small["gm_g"], small["gm_b"], small["lam_r"],
      small["lam_i"], small["d"], small["glu_b"], small["ln_g"], small["ln_b"])


def _ssm_discretize(lam_re, lam_im, log_step, b_re, b_im, c_re, c_im):
    depth = lam_re.shape[0]
    step = jnp.exp(log_step)[:, :, None]
    a, b = lam_re * step, lam_im * step

    def lam_pow(m):
        mag = jnp.exp(m * a)
        return mag * jnp.cos(m * b), mag * jnp.sin(m * b)

    def cmul(xr, xi, yr, yi):
        return xr * yr - xi * yi, xr * yi + xi * yr

    def group_block_diag(m, per_row, per_col):
        rows, cols = m.shape[-2], SSM_GROUPS * per_col
        lead = m.shape[:-2]
        m5 = m.reshape(lead + (rows // (SSM_GROUPS * per_row), SSM_GROUPS, per_row, per_col))
        g_i = lax.broadcasted_iota(jnp.int32, (SSM_GROUPS, per_col, cols), 0)
        p_i = lax.broadcasted_iota(jnp.int32, (SSM_GROUPS, per_col, cols), 1)
        q_i = lax.broadcasted_iota(jnp.int32, (SSM_GROUPS, per_col, cols), 2)
        place = (q_i == g_i * per_col + p_i).astype(F32)
        out = jnp.einsum('...ngrp,gpq->...ngrq', m5, place,
                         precision=lax.Precision.HIGHEST, preferred_element_type=F32)
        return out.astype(BF16).reshape(lead + (rows, cols))

    lbr, lbi = lam_pow(1)
    den = lam_re * lam_re + lam_im * lam_im
    fr = ((lbr - 1.0) * lam_re + lbi * lam_im) / den
    fi = (lbi * lam_re - (lbr - 1.0) * lam_im) / den
    bbr, bbi = cmul(fr[..., None], fi[..., None], b_re, b_im)

    coef = [cmul(*(q[..., None] for q in lam_pow(S5_STRIDE - 1 - j)), bbr, bbi)
            for j in range(S5_STRIDE)]
    w_in_side = jnp.stack([jnp.stack([c[ri] for c in coef], axis=1) for ri in range(2)],
                          axis=1)
    w_in_side = jnp.transpose(w_in_side, (0, 1, 2, 3, 5, 4)).reshape(
        depth, 2, S5_STRIDE * SSM_WIDTH, SSM_STATE)
    wbs = group_block_diag(w_in_side, SSM_GROUP, SSM_STATE)

    def c_times_lam_pow(m):
        pr, pi = lam_pow(m)
        return cmul(c_re, c_im, pr[:, :, None, :], pi[:, :, None, :])

    c_pows = [c_times_lam_pow(m) for m in range(S5_STRIDE)]
    w_out_side = jnp.stack([jnp.stack([cr, -ci], axis=1) for cr, ci in c_pows], axis=1)
    w_out_side = jnp.transpose(w_out_side, (0, 1, 2, 3, 5, 4)).reshape(
        depth, S5_STRIDE, 2, N_STATE, SSM_GROUP)
    wc_all = group_block_diag(w_out_side, SSM_STATE, SSM_GROUP).reshape(
        depth, S5_STRIDE, 2 * N_STATE, SSM_WIDTH)
    wc, wcp = wc_all[:, 0], wc_all[:, 1:]

    k_dir = jnp.stack([jnp.einsum('lgop,lgpi->lgio', cr, bbr)
                       - jnp.einsum('lgop,lgpi->lgio', ci, bbi)
                       for cr, ci in c_pows[:S5_STRIDE - 1]], axis=1)
    k_dir = group_block_diag(
        k_dir.reshape(depth, S5_STRIDE - 1, SSM_WIDTH, SSM_GROUP), SSM_GROUP, SSM_GROUP)
    kd = jnp.concatenate([k_dir[:, j - i] for j in range(S5_STRIDE - 1) for i in range(j + 1)],
                         axis=1)
    lsr, lsi = lam_pow(S5_STRIDE)
    return lsr.reshape(depth, N_STATE), lsi.reshape(depth, N_STATE), wbs, wc, wcp, kd


def kernel(x, mem, w_in, gm_w_s, gm_b_s, gm_ln_g, gm_ln_b, ssm_lam_re, ssm_lam_im,
           ssm_log_step, ssm_b_re, ssm_b_im, ssm_c_re, ssm_c_im, ssm_d, glu_w, glu_b,
           xa_w_k, xa_w_v, w_out, ln_g, ln_b):
    depth = w_in.shape[0]
    alpha = float((2 * depth) ** 0.25)
    lam_r, lam_i, wbs, wc, wcp, kd = _ssm_discretize(
        ssm_lam_re, ssm_lam_im, ssm_log_step, ssm_b_re, ssm_b_im, ssm_c_re, ssm_c_im)
    small = {
        "gm_w": gm_w_s,
        "gm_bias": jnp.repeat(jnp.swapaxes(gm_b_s, 1, 2), GM_HEAD_DIM, axis=2),
        "gm_g": gm_ln_g.reshape(depth, GM_WIDTH), "gm_b": gm_ln_b.reshape(depth, GM_WIDTH),
        "lam_r": lam_r, "lam_i": lam_i, "d": ssm_d, "glu_b": glu_b,
        "ln_g": ln_g, "ln_b": ln_b,
    }
    w_in16, w_out16 = w_in.astype(BF16), w_out.astype(BF16)
    glu_w16 = glu_w.astype(BF16)
    kt, vv = _kv_project(mem, jnp.swapaxes(xa_w_k, 1, 2).astype(BF16), xa_w_v.astype(BF16))
    for l in range(depth):
        big = {"w_in": w_in16, "w_out": w_out16, "wbs": wbs, "wc": wc[l],
               "wcp": wcp[l], "kd": kd[l], "glu_w": glu_w16[l]}
        x = _layer(x, l, kt, vv, big, small, alpha)
    return x
```

```python
import functools
import math

import jax
import jax.numpy as jnp
import numpy as np
from jax import lax
from jax.experimental import pallas as pl
from jax.experimental.pallas import tpu as pltpu

F32 = jnp.float32
BF16 = jnp.bfloat16

D_MODEL = 1024
GM_WIDTH = 512
GM_HEADS = 4
GM_HEAD_DIM = 128
CHUNK = 128
SSM_WIDTH = 256
SSM_GROUP = 16
SSM_GROUPS = 16
SSM_STATE = 64
N_STATE = SSM_GROUPS * SSM_STATE
XA_WIDTH = 256
XA_HEADS = 4
XA_HEAD_DIM = 64
IN_WIDTH = 3 * GM_WIDTH + 2 * SSM_WIDTH + 2 * XA_WIDTH
LN_EPS = 1e-5

C_U, C_V, C_GA = 0, GM_WIDTH, 2 * GM_WIDTH
C_XB = 3 * GM_WIDTH
C_GB = C_XB + SSM_WIDTH
C_Q = C_GB + SSM_WIDTH
C_GX = C_Q + XA_WIDTH
Y_A, Y_B, Y_X = 0, GM_WIDTH, GM_WIDTH + SSM_WIDTH

BLOCK_B = 8
BLOCK_T = CHUNK
ROWS = BLOCK_B * BLOCK_T
ROWS_PER_ITER = 2
S5_CHUNKS = 4
S5_STRIDE = 2
KD_OFFSETS = tuple(SSM_WIDTH * j * (j + 1) // 2 for j in range(S5_STRIDE - 1))
KD_ROWS = SSM_WIDTH * S5_STRIDE * (S5_STRIDE - 1) // 2
KV_BLOCK_B = 4
LANES = 128
PERM_PITCH = BLOCK_T + 8
VMEM_LIMIT_BYTES = 60 * 1024 * 1024

_LOG2E = 1.0 / math.log(2.0)
_GELU_K1 = -2.0 * math.sqrt(2.0 / math.pi) * _LOG2E
_GELU_K3 = _GELU_K1 * 0.044715


def _gelu(x):
    return x * (1.0 / (1.0 + jnp.exp2(x * (_GELU_K1 + _GELU_K3 * (x * x)))))


def _sigmoid(x):
    return 1.0 / (1.0 + jnp.exp2(x * (-_LOG2E)))


def _silu(x):
    return x * _sigmoid(x)


def _dot(a, b):
    return jnp.dot(a, b, preferred_element_type=F32)


def _aligned(v, m):
    return v if isinstance(v, int) else pl.multiple_of(v, m)


def _kv_kernel(mem_ref, wkt_ref, wv_ref, kt_ref, v_ref):
    depth = wkt_ref.shape[0]
    for b in range(KV_BLOCK_B):
        m16 = mem_ref[b].astype(BF16)
        for l in range(depth):
            kt = lax.dot_general(wkt_ref[l], m16, (((1,), (1,)), ((), ())),
                                 preferred_element_type=F32)
            kt_ref[l, b] = kt.astype(BF16)
            v_ref[l, b] = _dot(m16, wv_ref[l]).astype(BF16)


def _layer_kernel(alpha, layer,
                  x_ref, xn_ref, kt_ref, v_ref, w_in_ref, w_out_ref, wbs_ref, wca_ref,
                  kd_ref, gluw_ref,
                  gmw_ref, gmbias_ref, gmg_ref, gmb_ref, lamr_ref, lami_ref, d_ref,
                  glub_ref, lng_ref, lnb_ref,
                  o_ref,
                  z0_ref, z1_ref, wm_ref, y_ref, perm_ref, xbp_ref, hb_ref, gateb_ref):
    j = pl.program_id(1)
    lrow = slice(layer, layer + 1)
    z_refs = (z0_ref, z1_ref)
    assert len(z_refs) == ROWS_PER_ITER

    @pl.when(j == 0)
    def _():
        hb_ref[0:BLOCK_B, :] = jnp.zeros((BLOCK_B, 2 * N_STATE), F32)

    tri = (lax.broadcasted_iota(jnp.int32, (CHUNK, CHUNK), 0)
           >= lax.broadcasted_iota(jnp.int32, (CHUNK, CHUNK), 1))
    for hp in range(GM_HEADS // 2):
        wm_ref[hp] = jnp.concatenate(
            [jnp.where(tri, gmw_ref[2 * hp + i], 0.0).astype(BF16) for i in range(2)], axis=1)
    row_head = lax.broadcasted_iota(jnp.int32, (XA_WIDTH, XA_WIDTH), 0) // XA_HEAD_DIM
    col_head = lax.broadcasted_iota(jnp.int32, (XA_WIDTH, XA_WIDTH), 1) // XA_HEAD_DIM
    out_head = lax.broadcasted_iota(jnp.int32, (BLOCK_T, XA_WIDTH), 1) // XA_HEAD_DIM

    def in_proj_fills(row, z_ref, x_row=None):
        x16 = (x_ref[row] if x_row is None else x_row).astype(BF16)

        def chunk_uv():
            z_ref[:, C_U:C_GA] = _dot(x16, w_in_ref[:, C_U:C_GA])

        def chunk_ssm():
            zz = _dot(x16, w_in_ref[:, C_GA:C_Q])
            z_ref[:, C_GA:C_XB] = zz[:, 0:GM_WIDTH]
            xs = zz[:, C_XB - C_GA:C_GB - C_GA]
            p0 = _aligned(row * PERM_PITCH, 8)
            for s in range(SSM_WIDTH // LANES):
                perm_ref[s, pl.ds(p0, BLOCK_T), :] = xs[:, s * LANES:(s + 1) * LANES]
            r0 = _aligned(row * BLOCK_T, BLOCK_T)
            gateb_ref[pl.ds(r0, BLOCK_T), :] = _silu(zz[:, C_GB - C_GA:])

        def chunk_attn():
            z_ref[:, C_Q:] = _dot(x16, w_in_ref[:, C_Q:])

        return chunk_uv, chunk_ssm, chunk_attn

    def branches(row, z_ref, fills, early_fills):
        r0 = _aligned(row * BLOCK_T, BLOCK_T)
        q16 = (z_ref[:, C_Q:C_GX] * (XA_HEAD_DIM ** -0.5 * _LOG2E)).astype(BF16)
        kt = kt_ref[row]
        vv = v_ref[row]
        scores = [_dot(q16, jnp.where(row_head == h, kt, jnp.zeros_like(kt)))
                  for h in range(XA_HEADS)]
        for f in fills[:early_fills]:
            f()
        o = None
        scale = None
        for h in range(XA_HEADS):
            s = scores[h]
            p = jnp.exp2(s - jnp.max(s, axis=-1, keepdims=True))
            inv = 1.0 / jnp.sum(p, axis=-1, keepdims=True)
            oh = _dot(p.astype(BF16), jnp.where(col_head == h, vv, jnp.zeros_like(vv)))
            o = oh if o is None else o + oh
            scale = inv if scale is None else jnp.where(out_head == h, inv, scale)
        v = _gelu(z_ref[:, C_V:C_GA])
        vn16 = []
        for h in range(GM_HEADS):
            sl = slice(h * GM_HEAD_DIM, (h + 1) * GM_HEAD_DIM)
            vh = v[:, sl]
            mu = jnp.mean(vh, axis=-1, keepdims=True)
            xc = vh - mu
            var = jnp.mean(xc * xc, axis=-1, keepdims=True)
            vn = xc * lax.rsqrt(var + LN_EPS) * gmg_ref[lrow, sl] + gmb_ref[lrow, sl]
            vn16.append(vn.astype(BF16))
        mixed = []
        zero = jnp.zeros((CHUNK, GM_HEAD_DIM), BF16)
        for hp in range(GM_HEADS // 2):
            v_a, v_b = vn16[2 * hp], vn16[2 * hp + 1]
            rhs = jnp.concatenate([jnp.concatenate([v_a, zero], axis=1),
                                   jnp.concatenate([zero, v_b], axis=1)], axis=0)
            mixed.append(_dot(wm_ref[hp], rhs))
        for f in fills[early_fills:]:
            f()
        u = _gelu(z_ref[:, C_U:C_V])
        ya = u * (jnp.concatenate(mixed, axis=1) + gmbias_ref[...]) * _silu(z_ref[:, C_GA:C_XB])
        y_ref[pl.ds(r0, BLOCK_T), Y_A:Y_B] = ya.astype(BF16)
        yx = o * scale * _silu(z_ref[:, C_GX:])
        y_ref[pl.ds(r0, BLOCK_T), Y_X:] = yx.astype(BF16)

    n_z = len(z_refs)

    @pl.when((pl.program_id(0) == 0) & (j == 0))
    def _():
        for f in in_proj_fills(0, z_refs[0]):
            f()

    def row_group(g, carry):
        k0 = ROWS_PER_ITER * g
        for i in range(ROWS_PER_ITER):
            branches(k0 + i, z_refs[i % n_z], in_proj_fills(k0 + i + 1, z_refs[(i + 1) % n_z]), 3)
        return carry

    n_looped = (BLOCK_B - 1) // ROWS_PER_ITER * ROWS_PER_ITER
    lax.fori_loop(0, n_looped // ROWS_PER_ITER, row_group, 0)
    for k in range(n_looped, BLOCK_B - 1):
        branches(k, z_refs[k % n_z], in_proj_fills(k + 1, z_refs[(k + 1) % n_z]), 2)

    n_slab = SSM_WIDTH // LANES
    sup_steps = BLOCK_T // S5_STRIDE
    sup_rows = sup_steps * BLOCK_B

    for t in range(BLOCK_T):
        tp, jj = divmod(t, S5_STRIDE)
        for s in range(n_slab):
            lane0 = jj * SSM_WIDTH + s * LANES
            xbp_ref[tp * BLOCK_B:(tp + 1) * BLOCK_B, lane0:lane0 + LANES] = (
                perm_ref[s, pl.ds(t, BLOCK_B, stride=PERM_PITCH), :])

    def g_proj(ri):
        def f():
            hb_ref[BLOCK_B:BLOCK_B + sup_rows, ri * N_STATE:(ri + 1) * N_STATE] = _dot(
                xbp_ref[...].astype(BF16), wbs_ref[ri])
        return f

    branches(BLOCK_B - 1, z_refs[(BLOCK_B - 1) % n_z], (g_proj(0), g_proj(1)), 2)

    lam_r = jnp.broadcast_to(lamr_ref[lrow, :], (BLOCK_B, N_STATE))
    lam_i = jnp.broadcast_to(lami_ref[lrow, :], (BLOCK_B, N_STATE))

    steps = sup_steps // S5_CHUNKS
    chunk = steps * BLOCK_B
    state = [hb_ref[0:BLOCK_B, 0:N_STATE], hb_ref[0:BLOCK_B, N_STATE:]]

    def scan_chunk(c):
        hr, hi = state
        for tp in range(c * steps, (c + 1) * steps):
            rows = slice((tp + 1) * BLOCK_B, (tp + 2) * BLOCK_B)
            gr = hb_ref[rows, 0:N_STATE]
            gi = hb_ref[rows, N_STATE:]
            hr, hi = lam_r * hr - lam_i * hi + gr, lam_r * hi + lam_i * hr + gi
            hb_ref[rows, 0:N_STATE] = hr
            hb_ref[rows, N_STATE:] = hi
        state[0], state[1] = hr, hi

    def c_proj(c):
        r0 = c * chunk
        x_c = xbp_ref[r0:r0 + chunk, :]
        d = d_ref[lrow, :]
        h_prev = hb_ref[r0:r0 + chunk, :].astype(BF16)
        ys = []
        for jj in range(S5_STRIDE - 1):
            k0 = KD_OFFSETS[jj]
            n_in = (jj + 1) * SSM_WIDTH
            ys.append(_dot(h_prev, wca_ref[jj + 1])
                      + _dot(x_c[:, 0:n_in].astype(BF16), kd_ref[k0:k0 + n_in, :])
                      + d * x_c[:, jj * SSM_WIDTH:(jj + 1) * SSM_WIDTH])
        h_last = hb_ref[BLOCK_B + r0:BLOCK_B + r0 + chunk, :].astype(BF16)
        ys.append(_dot(h_last, wca_ref[0]) + d * x_c[:, (S5_STRIDE - 1) * SSM_WIDTH:])
        return ys

    def glu(c, ys):
        r0 = c * chunk
        for jj, y in enumerate(ys):
            yg = _gelu(y)
            xbp_ref[r0:r0 + chunk, jj * SSM_WIDTH:(jj + 1) * SSM_WIDTH] = (
                yg * _sigmoid(_dot(yg.astype(BF16), gluw_ref[...]) + glub_ref[lrow, :]))

    ys_prev = None
    for c in range(S5_CHUNKS):
        scan_chunk(c)
        ys_c = c_proj(c)
        if ys_prev is not None:
            glu(c - 1, ys_prev)
        ys_prev = ys_c
    glu(S5_CHUNKS - 1, ys_prev)
    hb_ref[0:BLOCK_B, 0:N_STATE] = state[0]
    hb_ref[0:BLOCK_B, N_STATE:] = state[1]

    for t in range(BLOCK_T):
        tp, jj = divmod(t, S5_STRIDE)
        for s in range(n_slab):
            lane0 = jj * SSM_WIDTH + s * LANES
            perm_ref[s, pl.ds(t, BLOCK_B, stride=PERM_PITCH), :] = (
                xbp_ref[tp * BLOCK_B:(tp + 1) * BLOCK_B, lane0:lane0 + LANES])

    for b in range(BLOCK_B):
        ybb = jnp.concatenate(
            [perm_ref[s, b * PERM_PITCH:b * PERM_PITCH + BLOCK_T, :] for s in range(n_slab)],
            axis=1)
        rows = slice(b * BLOCK_T, (b + 1) * BLOCK_T)
        y_ref[rows, Y_B:Y_X] = (ybb * gateb_ref[rows, :]).astype(BF16)

    for b in range(BLOCK_B):
        rows = slice(b * BLOCK_T, (b + 1) * BLOCK_T)
        r = alpha * x_ref[b] + _dot(y_ref[rows, :], w_out_ref[...])
        if b == BLOCK_B - 1:
            for f in in_proj_fills(0, z_refs[0], x_row=xn_ref[0]):
                f()
        mu = jnp.mean(r, axis=-1, keepdims=True)
        rc = r - mu
        var = jnp.mean(rc * rc, axis=-1, keepdims=True)
        o_ref[b] = rc * lax.rsqrt(var + LN_EPS) * lng_ref[lrow, :] + lnb_ref[lrow, :]


def _full(shape):
    nd = len(shape)
    return pl.BlockSpec(shape, lambda i, j, _nd=nd: (0,) * _nd)


def _of_layer(shape, layer):
    nd = len(shape)
    return pl.BlockSpec((None,) + tuple(shape),
                        lambda i, j, _l=layer, _nd=nd: (_l,) + (0,) * _nd)


def _kv_project(mem, wkt16, wv16):
    bsz, m_len, d = mem.shape
    depth = wkt16.shape[0]
    kt_sds = jax.ShapeDtypeStruct((depth, bsz, XA_WIDTH, m_len), BF16)
    v_sds = jax.ShapeDtypeStruct((depth, bsz, m_len, XA_WIDTH), BF16)
    return pl.pallas_call(
        _kv_kernel,
        grid=(bsz // KV_BLOCK_B,),
        in_specs=[pl.BlockSpec((KV_BLOCK_B, m_len, d), lambda i: (i, 0, 0)),
                  pl.BlockSpec((depth, XA_WIDTH, d), lambda i: (0, 0, 0)),
                  pl.BlockSpec((depth, d, XA_WIDTH), lambda i: (0, 0, 0))],
        out_specs=[pl.BlockSpec((depth, KV_BLOCK_B, XA_WIDTH, m_len), lambda i: (0, i, 0, 0)),
                   pl.BlockSpec((depth, KV_BLOCK_B, m_len, XA_WIDTH), lambda i: (0, i, 0, 0))],
        out_shape=[kt_sds, v_sds],
        name="kv_project",
    )(mem, wkt16, wv16)


def _layer(x, layer, kt, vv, big, small, alpha):
    bsz, seq, d = x.shape
    depth, _, _, m_len = kt.shape
    grid = (bsz // BLOCK_B, seq // BLOCK_T)
    n_steps = grid[0] * grid[1]

    def next_row0(i, j):
        s = jnp.minimum(i * grid[1] + j + 1, n_steps - 1)
        return (s // grid[1]) * BLOCK_B, s % grid[1], 0

    in_specs = [
        pl.BlockSpec((BLOCK_B, BLOCK_T, d), lambda i, j: (i, j, 0)),
        pl.BlockSpec((1, BLOCK_T, d), next_row0),
        pl.BlockSpec((None, BLOCK_B, XA_WIDTH, m_len), lambda i, j: (layer, i, 0, 0)),
        pl.BlockSpec((None, BLOCK_B, m_len, XA_WIDTH), lambda i, j: (layer, i, 0, 0)),
        _of_layer((d, IN_WIDTH), layer), _of_layer((d, d), layer),
        _of_layer((2, S5_STRIDE * SSM_WIDTH, N_STATE), layer),
        _of_layer((S5_STRIDE, 2 * N_STATE, SSM_WIDTH), layer),
        _of_layer((KD_ROWS, SSM_WIDTH), layer),
        _of_layer((SSM_WIDTH, SSM_WIDTH), layer),
        _of_layer((GM_HEADS, CHUNK, CHUNK), layer),
        _of_layer((CHUNK, GM_WIDTH), layer),
        _full((depth, GM_WIDTH)), _full((depth, GM_WIDTH)),
        _full((depth, N_STATE)), _full((depth, N_STATE)),
        _full((depth, SSM_WIDTH)), _full((depth, SSM_WIDTH)),
        _full((depth, d)), _full((depth, d)),
    ]
    scratch = [
        pltpu.VMEM((BLOCK_T, IN_WIDTH), F32),
        pltpu.VMEM((BLOCK_T, IN_WIDTH), F32),
        pltpu.VMEM((GM_HEADS // 2, CHUNK, 2 * CHUNK), BF16),
        pltpu.VMEM((ROWS, d), BF16),
        pltpu.VMEM((SSM_WIDTH // LANES, BLOCK_B * PERM_PITCH, LANES), F32),
        pltpu.VMEM((ROWS // S5_STRIDE, S5_STRIDE * SSM_WIDTH), F32),
        pltpu.VMEM((ROWS // S5_STRIDE + BLOCK_B, 2 * N_STATE), F32),
        pltpu.VMEM((ROWS, SSM_WIDTH), F32),
    ]
    return pl.pallas_call(
        functools.partial(_layer_kernel, alpha, layer),
        grid=grid,
        in_specs=in_specs,
        out_specs=pl.BlockSpec((BLOCK_B, BLOCK_T, d), lambda i, j: (i, j, 0)),
        out_shape=jax.ShapeDtypeStruct(x.shape, x.dtype),
        scratch_shapes=scratch,
        compiler_params=pltpu.CompilerParams(
            dimension_semantics=("arbitrary", "arbitrary"),
            vmem_limit_bytes=VMEM_LIMIT_BYTES),
        name="trunk_layer",
    )(x, x, kt, vv, big["w_in"], big["w_out"], big["wbs"], big["wca"], big["kd"],
      big["glu_w"],
      small["gm_w"], small["gm_bias"], small["gm_g"], small["gm_b"], small["lam_r"],
      small["lam_i"], small["d"], small["glu_b"], small["ln_g"], small["ln_b"])


def _ssm_discretize(lam_re, lam_im, log_step, b_re, b_im, c_re, c_im):
    depth = lam_re.shape[0]
    step = jnp.exp(log_step)[:, :, None]
    a, b = lam_re * step, lam_im * step

    def lam_pow(m):
        mag = jnp.exp(m * a)
        return mag * jnp.cos(m * b), mag * jnp.sin(m * b)

    def cmul(xr, xi, yr, yi):
        return xr * yr - xi * yi, xr * yi + xi * yr

    def group_block_diag(m, per_row, per_col):
        rows, cols = m.shape[-2], SSM_GROUPS * per_col
        lead = m.shape[:-2]
        m5 = m.reshape(lead + (rows // (SSM_GROUPS * per_row), SSM_GROUPS, per_row, per_col))
        g_i = lax.broadcasted_iota(jnp.int32, (SSM_GROUPS, per_col, cols), 0)
        p_i = lax.broadcasted_iota(jnp.int32, (SSM_GROUPS, per_col, cols), 1)
        q_i = lax.broadcasted_iota(jnp.int32, (SSM_GROUPS, per_col, cols), 2)
        place = (q_i == g_i * per_col + p_i).astype(F32)
        out = jnp.einsum('...ngrp,gpq->...ngrq', m5, place,
                         precision=lax.Precision.HIGHEST, preferred_element_type=F32)
        return out.astype(BF16).reshape(lead + (rows, cols))

    lbr, lbi = lam_pow(1)
    den = lam_re * lam_re + lam_im * lam_im
    fr = ((lbr - 1.0) * lam_re + lbi * lam_im) / den
    fi = (lbi * lam_re - (lbr - 1.0) * lam_im) / den
    bbr, bbi = cmul(fr[..., None], fi[..., None], b_re, b_im)

    coef = [cmul(*(q[..., None] for q in lam_pow(S5_STRIDE - 1 - j)), bbr, bbi)
            for j in range(S5_STRIDE)]
    w_in_side = jnp.stack([jnp.stack([c[ri] for c in coef], axis=1) for ri in range(2)],
                          axis=1)
    w_in_side = jnp.transpose(w_in_side, (0, 1, 2, 3, 5, 4)).reshape(
        depth, 2, S5_STRIDE * SSM_WIDTH, SSM_STATE)
    wbs = group_block_diag(w_in_side, SSM_GROUP, SSM_STATE)

    def c_times_lam_pow(m):
        pr, pi = lam_pow(m)
        return cmul(c_re, c_im, pr[:, :, None, :], pi[:, :, None, :])

    c_pows = [c_times_lam_pow(m) for m in range(S5_STRIDE)]
    w_out_side = jnp.stack([jnp.stack([cr, -ci], axis=1) for cr, ci in c_pows], axis=1)
    w_out_side = jnp.transpose(w_out_side, (0, 1, 2, 3, 5, 4)).reshape(
        depth, S5_STRIDE, 2, N_STATE, SSM_GROUP)
    wc_all = group_block_diag(w_out_side, SSM_STATE, SSM_GROUP).reshape(
        depth, S5_STRIDE, 2 * N_STATE, SSM_WIDTH)

    k_dir = jnp.stack([jnp.einsum('lgop,lgpi->lgio', cr, bbr)
                       - jnp.einsum('lgop,lgpi->lgio', ci, bbi)
                       for cr, ci in c_pows[:S5_STRIDE - 1]], axis=1)
    k_dir = group_block_diag(
        k_dir.reshape(depth, S5_STRIDE - 1, SSM_WIDTH, SSM_GROUP), SSM_GROUP, SSM_GROUP)
    kd = jnp.concatenate([k_dir[:, j - i] for j in range(S5_STRIDE - 1) for i in range(j + 1)],
                         axis=1)
    lsr, lsi = lam_pow(S5_STRIDE)
    return lsr.reshape(depth, N_STATE), lsi.reshape(depth, N_STATE), wbs, wc_all, kd


def kernel(x, mem, w_in, gm_w_s, gm_b_s, gm_ln_g, gm_ln_b, ssm_lam_re, ssm_lam_im,
           ssm_log_step, ssm_b_re, ssm_b_im, ssm_c_re, ssm_c_im, ssm_d, glu_w, glu_b,
           xa_w_k, xa_w_v, w_out, ln_g, ln_b):
    depth = w_in.shape[0]
    alpha = float((2 * depth) ** 0.25)
    lam_r, lam_i, wbs, wca, kd = _ssm_discretize(
        ssm_lam_re, ssm_lam_im, ssm_log_step, ssm_b_re, ssm_b_im, ssm_c_re, ssm_c_im)
    small = {
        "gm_w": gm_w_s,
        "gm_bias": jnp.repeat(jnp.swapaxes(gm_b_s, 1, 2), GM_HEAD_DIM, axis=2),
        "gm_g": gm_ln_g.reshape(depth, GM_WIDTH), "gm_b": gm_ln_b.reshape(depth, GM_WIDTH),
        "lam_r": lam_r, "lam_i": lam_i, "d": ssm_d, "glu_b": glu_b,
        "ln_g": ln_g, "ln_b": ln_b,
    }
    w_in16, w_out16 = w_in.astype(BF16), w_out.astype(BF16)
    glu_w16 = glu_w.astype(BF16)
    kt, vv = _kv_project(mem, jnp.swapaxes(xa_w_k, 1, 2).astype(BF16), xa_w_v.astype(BF16))
    for l in range(depth):
        big = {"w_in": w_in16, "w_out": w_out16, "wbs": wbs, "wca": wca,
               "kd": kd, "glu_w": glu_w16}
        x = _layer(x, l, kt, vv, big, small, alpha)
    return x
```
